```python
import math
import jax, jax.numpy as jnp
from jax import lax
import numpy as np


D_MODEL = 2048
BATCH = 2
SEQ = 16384
DEPTH = 2

SB_HEADS = 4
SB_HEAD_DIM = 256
SB_WIDTH = SB_HEADS * SB_HEAD_DIM
Q_BLOCK = 128
K_BLOCK = 128
GM_GROUPS = 4
GM_CHUNK = 128
GM_WIDTH = 512
GM_GROUP_DIM = GM_WIDTH // GM_GROUPS
S5_WIDTH = 512
S5_GROUP_DIM = 16
S5_GROUPS = S5_WIDTH // S5_GROUP_DIM
S5_STATE = 64
S5_DT_MIN = 1e-3
S5_DT_MAX = 1e-1
N_BRANCH = 3
MEM_LEN = 256
X_HEADS = 4
X_HEAD_DIM = 128
X_WIDTH = X_HEADS * X_HEAD_DIM
FFN_HIDDEN = ((8 * D_MODEL + 3 * 256 - 1) // (3 * 256)) * 256
RMS_EPS = 1e-6

OFF_Q = SB_WIDTH
OFF_K = OFF_Q + SB_WIDTH
OFF_V = OFF_K + SB_WIDTH
OFF_GM = OFF_V + 2 * GM_WIDTH
OFF_S5 = OFF_GM + S5_WIDTH
N_IN = OFF_S5 + N_BRANCH * D_MODEL

kernel_name = "hybrid_sb_gmlp_s5_gated_block"


def rms_norm(x, gain):
    xf = x.astype(jnp.float32)
    y = xf * lax.rsqrt(jnp.mean(xf * xf, axis=-1, keepdims=True) + RMS_EPS)
    return (y * gain.astype(jnp.float32)).astype(x.dtype)


def stick_breaking_attention(q, k, v):
    b, h, s, dh = q.shape
    nb = s // Q_BLOCK
    scale = dh ** -0.5
    kb_idx = jnp.arange(K_BLOCK)
    u_strict = (kb_idx[:, None] > kb_idx[None, :]).astype(jnp.float32)
    outs = []
    for i in range(nb):
        klen = (i + 1) * Q_BLOCK
        nkb = klen // K_BLOCK
        qi = q[:, :, i * Q_BLOCK:(i + 1) * Q_BLOCK]
        ki = k[:, :, :klen]
        vi = v[:, :, :klen]
        z = jnp.einsum('bhqd,bhkd->bhqk', qi, ki, preferred_element_type=jnp.float32) * scale
        qpos = i * Q_BLOCK + jnp.arange(Q_BLOCK)
        mask = jnp.arange(klen)[None, :] < qpos[:, None]
        ls_neg = jax.nn.log_sigmoid(-z)
        log_1m = jnp.where(mask, ls_neg, 0.0).reshape(b, h, Q_BLOCK, nkb, K_BLOCK)
        local = jnp.einsum('bhqcj,js->bhqcs', log_1m, u_strict)
        blk = jnp.sum(log_1m, axis=-1)
        c_idx = jnp.arange(nkb)
        m_strict = (c_idx[:, None] > c_idx[None, :]).astype(jnp.float32)
        carry = jnp.einsum('bhqc,cd->bhqd', blk, m_strict)
        log_w = (z + ls_neg).reshape(b, h, Q_BLOCK, nkb, K_BLOCK) + local + carry[..., None]
        w = jnp.where(mask, jnp.exp(log_w.reshape(b, h, Q_BLOCK, klen)), 0.0)
        o = jnp.einsum('bhqk,bhkd->bhqd', w.astype(v.dtype), vi, preferred_element_type=jnp.float32)
        outs.append(o.astype(v.dtype))
    return jnp.concatenate(outs, axis=2)


def spatial_gating(uv, v_gain, w_s, b_s):
    u, v = jnp.split(uv, 2, axis=-1)
    v = rms_norm(v, v_gain)
    b, s, _ = v.shape
    vc = v.reshape(b, s // GM_CHUNK, GM_CHUNK, GM_GROUPS, GM_GROUP_DIM)
    causal = jnp.tril(jnp.ones((GM_CHUNK, GM_CHUNK), dtype=bool))
    w = jnp.where(causal[None], w_s, 0.0).astype(v.dtype)
    mixed = jnp.einsum('gts,bcsgd->bctgd', w, vc) + b_s.T.astype(v.dtype)[None, None, :, :, None]
    return u * mixed.reshape(b, s, GM_WIDTH)


def _ssm_combine(left, right):
    a1r, a1i, b1r, b1i = left
    a2r, a2i, b2r, b2i = right
    return (a2r * a1r - a2i * a1i,
            a2r * a1i + a2i * a1r,
            a2r * b1r - a2i * b1i + b2r,
            a2r * b1i + a2i * b1r + b2i)


def s5_layer(u, lam_re, lam_im, log_dt, b_re, b_im, c_re, c_im, d_skip, w_glu):
    bsz, s, _ = u.shape
    f32 = jnp.float32
    ug = u.reshape(bsz, s, S5_GROUPS, S5_GROUP_DIM).astype(f32)
    lr, li = lam_re.astype(f32), lam_im.astype(f32)
    dt = jnp.exp(log_dt.astype(f32))[:, None]
    mag = jnp.exp(lr * dt)
    a_re, a_im = mag * jnp.cos(li * dt), mag * jnp.sin(li * dt)
    den = lr * lr + li * li
    x_ = a_re - 1.0
    f_re = (x_ * lr + a_im * li) / den
    f_im = (a_im * lr - x_ * li) / den
    br, bi = b_re.astype(f32), b_im.astype(f32)
    bb_re = f_re[..., None] * br - f_im[..., None] * bi
    bb_im = f_re[..., None] * bi + f_im[..., None] * br
    bu_re = jnp.einsum('bsgc,gnc->bsgn', ug, bb_re)
    bu_im = jnp.einsum('bsgc,gnc->bsgn', ug, bb_im)
    shp = (1, s, S5_GROUPS, S5_STATE)
    ar = jnp.broadcast_to(a_re[None, None], shp)
    ai = jnp.broadcast_to(a_im[None, None], shp)
    _, _, x_re, x_im = lax.associative_scan(_ssm_combine, (ar, ai, bu_re, bu_im), axis=1)
    y = (jnp.einsum('bsgn,gcn->bsgc', x_re, c_re.astype(f32))
         - jnp.einsum('bsgn,gcn->bsgc', x_im, c_im.astype(f32))
         + d_skip.astype(f32).reshape(S5_GROUPS, S5_GROUP_DIM) * ug)
    y = jax.nn.gelu(y.reshape(bsz, s, S5_WIDTH)).astype(u.dtype)
    a, g = jnp.split(y @ w_glu, 2, axis=-1)
    return a * jax.nn.sigmoid(g)


def memory_cross_attention(h, mem_n, w_q, w_kv, w_o):
    b, s, _ = h.shape
    q = (h @ w_q).reshape(b, s, X_HEADS, X_HEAD_DIM)
    k, v = jnp.split(mem_n @ w_kv, 2, axis=-1)
    k = k.reshape(b, MEM_LEN, X_HEADS, X_HEAD_DIM)
    v = v.reshape(b, MEM_LEN, X_HEADS, X_HEAD_DIM)
    z = jnp.einsum('bshd,bmhd->bhsm', q, k, preferred_element_type=jnp.float32) * (X_HEAD_DIM ** -0.5)
    p = jax.nn.softmax(z, axis=-1).astype(v.dtype)
    o = jnp.einsum('bhsm,bmhd->bshd', p, v)
    return o.reshape(b, s, X_WIDTH) @ w_o


def setup_inputs(seed: int = 0) -> dict:
    key = jax.random.key(seed)
    ks = jax.random.split(key, 32)
    L, D = DEPTH, D_MODEL

    def nrm(k, shape, fan_in):
        return jax.random.normal(k, shape, jnp.float32) * (fan_in ** -0.5)

    def gain(k, shape):
        return 1.0 + 0.02 * jax.random.normal(k, shape, jnp.float32)

    n_idx = jnp.arange(S5_STATE, dtype=jnp.float32)
    return {
        "x": jax.random.normal(ks[0], (BATCH, SEQ, D), jnp.float32),
        "mem": jax.random.normal(ks[1], (BATCH, MEM_LEN, D), jnp.float32),
        "mix_pre_gain": gain(ks[2], (L, D)),
        "mix_post_gain": gain(ks[3], (L, D)),
        "w_in": nrm(ks[4], (L, D, N_IN), D),
        "gm_v_gain": gain(ks[5], (L, GM_WIDTH)),
        "gm_w_s": nrm(ks[6], (L, GM_GROUPS, GM_CHUNK, GM_CHUNK), GM_CHUNK),
        "gm_b_s": gain(ks[7], (L, GM_GROUPS, GM_CHUNK)),
        "s5_lam_re": -0.5 + 0.01 * jax.random.normal(ks[8], (L, S5_GROUPS, S5_STATE), jnp.float32),
        "s5_lam_im": jnp.pi * n_idx + 0.01 * jax.random.normal(ks[9], (L, S5_GROUPS, S5_STATE), jnp.float32),
        "s5_log_dt": jax.random.uniform(ks[10], (L, S5_GROUPS), jnp.float32,
                                        minval=math.log(S5_DT_MIN), maxval=math.log(S5_DT_MAX)),
        "s5_b_re": nrm(ks[11], (L, S5_GROUPS, S5_STATE, S5_GROUP_DIM), 2 * S5_GROUP_DIM),
        "s5_b_im": nrm(ks[12], (L, S5_GROUPS, S5_STATE, S5_GROUP_DIM), 2 * S5_GROUP_DIM),
        "s5_c_re": nrm(ks[13], (L, S5_GROUPS, S5_GROUP_DIM, S5_STATE), 2 * S5_STATE),
        "s5_c_im": nrm(ks[14], (L, S5_GROUPS, S5_GROUP_DIM, S5_STATE), 2 * S5_STATE),
        "s5_d": jax.random.normal(ks[15], (L, S5_WIDTH), jnp.float32),
        "s5_w_glu": nrm(ks[16], (L, S5_WIDTH, 2 * S5_WIDTH), S5_WIDTH),
        "w_br_a": nrm(ks[17], (L, SB_WIDTH, D), SB_WIDTH),
        "w_br_b": nrm(ks[18], (L, GM_WIDTH, D), GM_WIDTH),
        "w_br_c": nrm(ks[19], (L, S5_WIDTH, D), S5_WIDTH),
        "w_out": nrm(ks[20], (L, D, D), D),
        "xattn_pre_gain": gain(ks[21], (L, D)),
        "xattn_post_gain": gain(ks[22], (L, D)),
        "mem_gain": gain(ks[23], (L, D)),
        "w_xq": nrm(ks[24], (L, D, X_WIDTH), D),
        "w_xkv": nrm(ks[25], (L, D, 2 * X_WIDTH), D),
        "w_xo": nrm(ks[26], (L, X_WIDTH, D), X_WIDTH),
        "ffn_pre_gain": gain(ks[27], (L, D)),
        "ffn_post_gain": gain(ks[28], (L, D)),
        "w_ffn_in": nrm(ks[29], (L, D, 2 * FFN_HIDDEN), D),
        "w_ffn_out": nrm(ks[30], (L, FFN_HIDDEN, D), FFN_HIDDEN),
    }


def reference(x, mem, mix_pre_gain, mix_post_gain, w_in, gm_v_gain, gm_w_s, gm_b_s,
              s5_lam_re, s5_lam_im, s5_log_dt, s5_b_re, s5_b_im, s5_c_re, s5_c_im, s5_d,
              s5_w_glu, w_br_a, w_br_b, w_br_c, w_out, xattn_pre_gain, xattn_post_gain,
              mem_gain, w_xq, w_xkv, w_xo, ffn_pre_gain, ffn_post_gain, w_ffn_in, w_ffn_out):
    b, s, d = x.shape
    for l in range(DEPTH):
        h = rms_norm(x, mix_pre_gain[l])
        proj = h @ w_in[l]
        q, k, v, gm_uv, s5_u, gates = jnp.split(proj, [OFF_Q, OFF_K, OFF_V, OFF_GM, OFF_S5], axis=-1)
        to_heads = lambda t: t.reshape(b, s, SB_HEADS, SB_HEAD_DIM).transpose(0, 2, 1, 3)
        y_a = stick_breaking_attention(to_heads(q), to_heads(k), to_heads(v))
        y_a = y_a.transpose(0, 2, 1, 3).reshape(b, s, SB_WIDTH) @ w_br_a[l]
        y_b = spatial_gating(jax.nn.gelu(gm_uv), gm_v_gain[l], gm_w_s[l], gm_b_s[l]) @ w_br_b[l]
        y_c = s5_layer(s5_u, s5_lam_re[l], s5_lam_im[l], s5_log_dt[l], s5_b_re[l], s5_b_im[l],
                       s5_c_re[l], s5_c_im[l], s5_d[l], s5_w_glu[l]) @ w_br_c[l]
        g = jax.nn.sigmoid(gates).reshape(b, s, N_BRANCH, d)
        merged = g[:, :, 0] * y_a + g[:, :, 1] * y_b + g[:, :, 2] * y_c
        x = x + rms_norm(merged @ w_out[l], mix_post_gain[l])
        h = rms_norm(x, xattn_pre_gain[l])
        m = rms_norm(mem, mem_gain[l])
        x = x + rms_norm(memory_cross_attention(h, m, w_xq[l], w_xkv[l], w_xo[l]), xattn_post_gain[l])
        h = rms_norm(x, ffn_pre_gain[l])
        gt, up = jnp.split(h @ w_ffn_in[l], 2, axis=-1)
        x = x + rms_norm((jax.nn.silu(gt) * up) @ w_ffn_out[l], ffn_post_gain[l])
    return x
```

```python
import functools
import math

import jax
import jax.numpy as jnp
from jax import lax
from jax.experimental import pallas as pl
from jax.experimental.pallas import tpu as pltpu

F32 = jnp.float32
BF16 = jnp.bfloat16

RMS_EPS = 1e-6

SB_HEADS = 4
SB_HEAD_DIM = 256
SB_WIDTH = SB_HEADS * SB_HEAD_DIM
GM_GROUPS = 4
GM_CHUNK = 128
GM_WIDTH = 512
S5_WIDTH = 512
S5_GROUP_DIM = 16
S5_GROUPS = S5_WIDTH // S5_GROUP_DIM
S5_STATE = 64
S5_LANES = S5_GROUPS * S5_STATE
X_HEADS = 4
X_HEAD_DIM = 128
X_WIDTH = X_HEADS * X_HEAD_DIM

OFF_K = SB_WIDTH
OFF_V = 2 * SB_WIDTH
OFF_GM = 3 * SB_WIDTH
OFF_S5 = OFF_GM + 2 * GM_WIDTH
OFF_GATE = OFF_S5 + S5_WIDTH

SUBLANES = 8
LANES = 128

EXP_ZERO_BELOW = -104.0

VMEM_LIMIT = 56 * 1024 * 1024


def _cparams(*sem):
    return pltpu.CompilerParams(dimension_semantics=sem, vmem_limit_bytes=VMEM_LIMIT)


def _rms(xf, gain):
    ms = jnp.mean(xf * xf, axis=-1, keepdims=True)
    return xf * lax.rsqrt(ms + RMS_EPS) * gain


def _dot(a, b):
    return jnp.dot(a, b, preferred_element_type=F32)


def _dot_nt(a, b):
    return lax.dot_general(a, b, (((1,), (1,)), ((), ())), preferred_element_type=F32)


def _norm_matmul_kernel(x_ref, g_ref, w_ref, o_ref, h_ref):
    @pl.when(pl.program_id(1) == 0)
    def _():
        h_ref[...] = _rms(x_ref[...], g_ref[...]).astype(BF16)

    o_ref[...] = _dot(h_ref[...], w_ref[...]).astype(o_ref.dtype)


def norm_matmul(x, gain, w, *, bm, bn):
    t, d = x.shape
    n = w.shape[1]
    return pl.pallas_call(
        _norm_matmul_kernel,
        grid=(t // bm, n // bn),
        in_specs=[
            pl.BlockSpec((bm, d), lambda i, j: (i, 0)),
            pl.BlockSpec((1, d), lambda i, j: (0, 0)),
            pl.BlockSpec((d, bn), lambda i, j: (0, j)),
        ],
        out_specs=pl.BlockSpec((bm, bn), lambda i, j: (i, j)),
        out_shape=jax.ShapeDtypeStruct((t, n), BF16),
        scratch_shapes=[pltpu.VMEM((bm, d), BF16)],
        compiler_params=_cparams("parallel", "arbitrary"),
        name="norm_matmul",
    )(x, gain.reshape(1, d), w)


def _sb_tile(z, mask, carry, v, uo, bk):
    sp = jnp.log1p(jnp.exp(-jnp.abs(z)))
    ls_neg = jnp.minimum(-z, 0.0) - sp
    ls_pos = jnp.minimum(z, 0.0) - sp
    if mask is not None:
        ls_neg = jnp.where(mask, ls_neg, 0.0)
    hi = ls_neg.astype(BF16)
    lo = (ls_neg - hi.astype(F32)).astype(BF16)
    cs = _dot(hi, uo) + _dot(lo, uo)
    log_w = ls_pos + cs[:, :bk] + carry
    w = jnp.exp(log_w)
    if mask is not None:
        w = jnp.where(mask, w, 0.0)
    return _dot(w.astype(BF16), v), carry + cs[:, bk:]


def _sb_attn_kernel(q_ref, k_ref, v_ref, uo_ref, o_ref, acc_ref, carry_ref, *, blk, scale):
    i = pl.program_id(2)
    q = q_ref[...]
    uo = uo_ref[...]

    off = pl.multiple_of(i * blk, blk)
    z = _dot_nt(q, k_ref[pl.ds(off, blk), :]) * scale
    row = lax.broadcasted_iota(jnp.int32, (blk, blk), 0)
    col = lax.broadcasted_iota(jnp.int32, (blk, blk), 1)
    pv, carry = _sb_tile(z, col < row, jnp.zeros((blk, LANES), F32), v_ref[pl.ds(off, blk), :], uo, blk)
    acc_ref[...] = pv
    carry_ref[...] = carry

    def cond(state):
        kb, live = state
        return jnp.logical_and(kb >= 0, live > EXP_ZERO_BELOW)

    def body(state):
        kb, _ = state
        o = pl.multiple_of(kb * blk, blk)
        zz = _dot_nt(q, k_ref[pl.ds(o, blk), :]) * scale
        pv_, carry_ = _sb_tile(zz, None, carry_ref[...], v_ref[pl.ds(o, blk), :], uo, blk)
        acc_ref[...] += pv_
        carry_ref[...] = carry_
        return kb - 1, jnp.max(carry_)

    lax.while_loop(cond, body, (i - 1, jnp.max(carry)))
    o_ref[...] = acc_ref[...].astype(o_ref.dtype)


def sb_attention(proj, batch, seq, *, blk=128):
    t = proj.shape[0]
    nq = seq // blk
    dh = SB_HEAD_DIM
    r = lax.broadcasted_iota(jnp.int32, (blk, blk + LANES), 0)
    c = lax.broadcasted_iota(jnp.int32, (blk, blk + LANES), 1)
    uo = jnp.where(jnp.logical_or(r > c, c >= blk), 1.0, 0.0).astype(BF16)
    kernel = functools.partial(_sb_attn_kernel, blk=blk, scale=dh ** -0.5)
    return pl.pallas_call(
        kernel,
        grid=(batch, SB_HEADS, nq),
        in_specs=[
            pl.BlockSpec((blk, dh), lambda b, h, i: (b * nq + i, h)),
            pl.BlockSpec((seq, dh), lambda b, h, i: (b, OFF_K // dh + h)),
            pl.BlockSpec((seq, dh), lambda b, h, i: (b, OFF_V // dh + h)),
            pl.BlockSpec((blk, blk + LANES), lambda b, h, i: (0, 0)),
        ],
        out_specs=pl.BlockSpec((blk, dh), lambda b, h, i: (b * nq + i, h)),
        out_shape=jax.ShapeDtypeStruct((t, SB_WIDTH), BF16),
        scratch_shapes=[pltpu.VMEM((blk, dh), F32), pltpu.VMEM((blk, LANES), F32)],
        compiler_params=_cparams("parallel", "parallel", "arbitrary"),
        name="sb_attention",
    )(proj, proj, proj, uo)


def _gm_kernel(uv_ref, gain_ref, w_ref, b_ref, o_ref, *, chunks):
    uv = jax.nn.gelu(uv_ref[...].astype(F32))
    u = uv[:, :GM_WIDTH]
    v = _rms(uv[:, GM_WIDTH:], gain_ref[...]).astype(BF16)
    row = lax.broadcasted_iota(jnp.int32, (GM_CHUNK, GM_CHUNK), 0)
    col = lax.broadcasted_iota(jnp.int32, (GM_CHUNK, GM_CHUNK), 1)
    gd = GM_WIDTH // GM_GROUPS
    for g in range(GM_GROUPS):
        wg = jnp.where(col <= row, w_ref[g], 0.0).astype(BF16)
        bg = b_ref[g]
        for c in range(chunks):
            rs = slice(c * GM_CHUNK, (c + 1) * GM_CHUNK)
            cs = slice(g * gd, (g + 1) * gd)
            mixed = _dot(wg, v[rs, cs]) + bg
            o_ref[rs, cs] = (u[rs, cs] * mixed).astype(o_ref.dtype)


def spatial_gating(proj, v_gain, w_s, b_s, *, bm):
    t = proj.shape[0]
    gd = GM_WIDTH // GM_GROUPS
    b_full = jnp.broadcast_to(b_s[:, :, None], (GM_GROUPS, GM_CHUNK, gd)).astype(F32)
    kernel = functools.partial(_gm_kernel, chunks=bm // GM_CHUNK)
    return pl.pallas_call(
        kernel,
        grid=(t // bm,),
        in_specs=[
            pl.BlockSpec((bm, 2 * GM_WIDTH), lambda i: (i, OFF_GM // (2 * GM_WIDTH))),
            pl.BlockSpec((1, GM_WIDTH), lambda i: (0, 0)),
            pl.BlockSpec((GM_GROUPS, GM_CHUNK, GM_CHUNK), lambda i: (0, 0, 0)),
            pl.BlockSpec((GM_GROUPS, GM_CHUNK, gd), lambda i: (0, 0, 0)),
        ],
        out_specs=pl.BlockSpec((bm, GM_WIDTH), lambda i: (i, 0)),
        out_shape=jax.ShapeDtypeStruct((t, GM_WIDTH), BF16),
        compiler_params=_cparams("parallel"),
        name="spatial_gating",
    )(proj, v_gain.reshape(1, GM_WIDTH), w_s, b_full)


def _s5_kernel(u_ref, wb_ref, pw_ref, wc_ref, d_ref, wg_ref, o_ref, x_ref, carry_ref, *, ts):
    n = S5_LANES

    @pl.when(pl.program_id(1) == 0)
    def _():
        carry_ref[...] = jnp.zeros_like(carry_ref)

    u = u_ref[...]
    x_ref[...] = _dot(u, wb_ref[...])

    def body(r, c):
        off = pl.multiple_of(r * SUBLANES, SUBLANES)
        rows = pl.ds(off, SUBLANES)
        for lc in range(n // LANES):
            re = pl.ds(lc * LANES, LANES)
            im = pl.ds(n + lc * LANES, LANES)
            xr = x_ref[rows, re]
            xi = x_ref[rows, im]
            for step, shift in enumerate((1, 2, 4)):
                pr = pw_ref[step, :, re]
                pi = pw_ref[step, :, im]
                sr = pltpu.roll(xr, shift, 0)
                si = pltpu.roll(xi, shift, 0)
                xr, xi = xr + (pr * sr - pi * si), xi + (pr * si + pi * sr)
            ar = pw_ref[3, :, re]
            ai = pw_ref[3, :, im]
            cr = carry_ref[:, re]
            ci = carry_ref[:, im]
            xr = xr + (ar * cr - ai * ci)
            xi = xi + (ar * ci + ai * cr)
            x_ref[rows, re] = xr
            x_ref[rows, im] = xi
            carry_ref[:, re] = jnp.broadcast_to(xr[SUBLANES - 1:, :], (SUBLANES, LANES))
            carry_ref[:, im] = jnp.broadcast_to(xi[SUBLANES - 1:, :], (SUBLANES, LANES))
        return c

    lax.fori_loop(0, ts // SUBLANES, body, 0)

    y = _dot(x_ref[...].astype(BF16), wc_ref[...]) + d_ref[...] * u.astype(F32)
    y = jax.nn.gelu(y).astype(BF16)
    ag = _dot(y, wg_ref[...])
    o_ref[...] = (ag[:, :S5_WIDTH] * jax.nn.sigmoid(ag[:, S5_WIDTH:])).astype(o_ref.dtype)


def _s5_tables(lam_re, lam_im, log_dt, b_re, b_im, c_re, c_im):
    g, n, c = S5_GROUPS, S5_STATE, S5_GROUP_DIM
    lr, li = lam_re.astype(F32), lam_im.astype(F32)
    dt = jnp.exp(log_dt.astype(F32))[:, None]
    mag = jnp.exp(lr * dt)
    a_re, a_im = mag * jnp.cos(li * dt), mag * jnp.sin(li * dt)
    den = lr * lr + li * li
    x_ = a_re - 1.0
    f_re = (x_ * lr + a_im * li) / den
    f_im = (a_im * lr - x_ * li) / den
    br, bi = b_re.astype(F32), b_im.astype(F32)
    bb_re = f_re[..., None] * br - f_im[..., None] * bi
    bb_im = f_re[..., None] * bi + f_im[..., None] * br
    eye = jnp.eye(g, dtype=F32)
    wb_re = jnp.einsum('gnc,gh->gchn', bb_re, eye).reshape(g * c, g * n)
    wb_im = jnp.einsum('gnc,gh->gchn', bb_im, eye).reshape(g * c, g * n)
    wb = jnp.concatenate([wb_re, wb_im], axis=1).astype(BF16)
    wc_re = jnp.einsum('gcn,gh->gnhc', c_re.astype(F32), eye).reshape(g * n, g * c)
    wc_im = jnp.einsum('gcn,gh->gnhc', c_im.astype(F32), eye).reshape(g * n, g * c)
    wc = jnp.concatenate([wc_re, -wc_im], axis=0).astype(BF16)
    ar, ai = a_re.reshape(-1), a_im.reshape(-1)

    def cmul(p, q):
        return p[0] * q[0] - p[1] * q[1], p[0] * q[1] + p[1] * q[0]

    pows = [(ar, ai)]
    for _ in range(SUBLANES - 1):
        pows.append(cmul(pows[-1], (ar, ai)))
    rows = jnp.arange(SUBLANES)[:, None]
    tabs = []
    for shift in (1, 2, 4):
        pr, pi = pows[shift - 1]
        keep = rows >= shift
        tabs.append(jnp.concatenate([jnp.where(keep, pr[None], 0.0), jnp.where(keep, pi[None], 0.0)], axis=1))
    tabs.append(jnp.concatenate([jnp.stack([p[0] for p in pows]), jnp.stack([p[1] for p in pows])], axis=1))
    return wb, wc, jnp.stack(tabs).astype(F32)


def s5_mixer(proj, batch, seq, wb, wc, pw, d_skip, w_glu, *, ts):
    t = proj.shape[0]
    nt = seq // ts
    n2 = 2 * S5_LANES
    kernel = functools.partial(_s5_kernel, ts=ts)
    return pl.pallas_call(
        kernel,
        grid=(batch, nt),
        in_specs=[
            pl.BlockSpec((ts, S5_WIDTH), lambda b, i: (b * nt + i, OFF_S5 // S5_WIDTH)),
            pl.BlockSpec((S5_WIDTH, n2), lambda b, i: (0, 0)),
            pl.BlockSpec((4, SUBLANES, n2), lambda b, i: (0, 0, 0)),
            pl.BlockSpec((n2, S5_WIDTH), lambda b, i: (0, 0)),
            pl.BlockSpec((1, S5_WIDTH), lambda b, i: (0, 0)),
            pl.BlockSpec((S5_WIDTH, 2 * S5_WIDTH), lambda b, i: (0, 0)),
        ],
        out_specs=pl.BlockSpec((ts, S5_WIDTH), lambda b, i: (b * nt + i, 0)),
        out_shape=jax.ShapeDtypeStruct((t, S5_WIDTH), BF16),
        scratch_shapes=[pltpu.VMEM((ts, n2), F32), pltpu.VMEM((SUBLANES, n2), F32)],
        compiler_params=_cparams("parallel", "arbitrary"),
        name="s5_mixer",
    )(proj, wb, pw, wc, d_skip.reshape(1, S5_WIDTH).astype(F32), w_glu)


def _merge_kernel(ya_ref, yb_ref, yc_ref, g0_ref, g1_ref, g2_ref, x_ref, wa_ref, wb_ref, wc_ref,
                  wo_ref, gain_ref, o_ref, m_ref, *, nc, cw):
    c = pl.program_id(1)
    m = (jax.nn.sigmoid(g0_ref[...].astype(F32)) * _dot(ya_ref[...], wa_ref[...])
         + jax.nn.sigmoid(g1_ref[...].astype(F32)) * _dot(yb_ref[...], wb_ref[...])
         + jax.nn.sigmoid(g2_ref[...].astype(F32)) * _dot(yc_ref[...], wc_ref[...]))
    m_ref[c] = m.astype(BF16)

    @pl.when(c == nc - 1)
    def _():
        y = _dot(m_ref[0], wo_ref[0:cw, :])
        for k in range(1, nc):
            y += _dot(m_ref[k], wo_ref[k * cw:(k + 1) * cw, :])
        o_ref[...] = x_ref[...] + _rms(y, gain_ref[...])


def merge_out(y_a, y_b, y_c, proj, x, w_a, w_b, w_c, w_o, gain, *, bm, cw=512):
    t, d = x.shape
    nc = d // cw
    g0 = OFF_GATE // cw
    kernel = functools.partial(_merge_kernel, nc=nc, cw=cw)
    return pl.pallas_call(
        kernel,
        grid=(t // bm, nc),
        in_specs=[
            pl.BlockSpec((bm, SB_WIDTH), lambda i, c: (i, 0)),
            pl.BlockSpec((bm, GM_WIDTH), lambda i, c: (i, 0)),
            pl.BlockSpec((bm, S5_WIDTH), lambda i, c: (i, 0)),
            pl.BlockSpec((bm, cw), lambda i, c: (i, g0 + c)),
            pl.BlockSpec((bm, cw), lambda i, c: (i, g0 + nc + c)),
            pl.BlockSpec((bm, cw), lambda i, c: (i, g0 + 2 * nc + c)),
            pl.BlockSpec((bm, d), lambda i, c: (i, 0)),
            pl.BlockSpec((SB_WIDTH, cw), lambda i, c: (0, c)),
            pl.BlockSpec((GM_WIDTH, cw), lambda i, c: (0, c)),
            pl.BlockSpec((S5_WIDTH, cw), lambda i, c: (0, c)),
            pl.BlockSpec((d, d), lambda i, c: (0, 0)),
            pl.BlockSpec((1, d), lambda i, c: (0, 0)),
        ],
        out_specs=pl.BlockSpec((bm, d), lambda i, c: (i, 0)),
        out_shape=jax.ShapeDtypeStruct((t, d), F32),
        scratch_shapes=[pltpu.VMEM((nc, bm, cw), BF16)],
        compiler_params=_cparams("parallel", "arbitrary"),
        name="merge_out",
    )(y_a, y_b, y_c, proj, proj, proj, x, w_a, w_b, w_c, w_o, gain.reshape(1, d))


def _xattn_kernel(x_ref, g_ref, wq_ref, kv_ref, wo_ref, pg_ref, o_ref, *, scale):
    x = x_ref[...]
    h = _rms(x, g_ref[...]).astype(BF16)
    q = (_dot(h, wq_ref[...]) * scale).astype(BF16)
    outs = []
    for hd in range(X_HEADS):
        cs = slice(hd * X_HEAD_DIM, (hd + 1) * X_HEAD_DIM)
        vs = slice(X_WIDTH + hd * X_HEAD_DIM, X_WIDTH + (hd + 1) * X_HEAD_DIM)
        z = _dot_nt(q[:, cs], kv_ref[:, cs])
        e = jnp.exp(z - jnp.max(z, axis=-1, keepdims=True))
        p = e / jnp.sum(e, axis=-1, keepdims=True)
        outs.append(_dot(p.astype(BF16), kv_ref[:, vs]))
    o = jnp.concatenate(outs, axis=-1).astype(BF16)
    o_ref[...] = x + _rms(_dot(o, wo_ref[...]), pg_ref[...])


def cross_attention(x, kv, seq, gain, w_q, w_o, post_gain, *, bm):
    t, d = x.shape
    mem_len = kv.shape[0] // (t // seq)
    per_batch = seq // bm
    kernel = functools.partial(_xattn_kernel, scale=X_HEAD_DIM ** -0.5)
    return pl.pallas_call(
        kernel,
        grid=(t // bm,),
        in_specs=[
            pl.BlockSpec((bm, d), lambda i: (i, 0)),
            pl.BlockSpec((1, d), lambda i: (0, 0)),
            pl.BlockSpec((d, X_WIDTH), lambda i: (0, 0)),
            pl.BlockSpec((mem_len, 2 * X_WIDTH), lambda i: (i // per_batch, 0)),
            pl.BlockSpec((X_WIDTH, d), lambda i: (0, 0)),
            pl.BlockSpec((1, d), lambda i: (0, 0)),
        ],
        out_specs=pl.BlockSpec((bm, d), lambda i: (i, 0)),
        out_shape=jax.ShapeDtypeStruct((t, d), F32),
        compiler_params=_cparams("parallel"),
        name="cross_attention",
    )(x, gain.reshape(1, d), w_q, kv, w_o, post_gain.reshape(1, d))


def _ffn_kernel(x_ref, g_ref, wg_ref, wu_ref, wo_ref, pg_ref, o_ref, h_ref, acc_ref, *, nf):
    f = pl.program_id(1)

    @pl.when(f == 0)
    def _():
        h_ref[...] = _rms(x_ref[...], g_ref[...]).astype(BF16)
        acc_ref[...] = jnp.zeros_like(acc_ref)

    h = h_ref[...]
    a = (jax.nn.silu(_dot(h, wg_ref[...])) * _dot(h, wu_ref[...])).astype(BF16)
    acc_ref[...] += _dot(a, wo_ref[...])

    @pl.when(f == nf - 1)
    def _():
        o_ref[...] = x_ref[...] + _rms(acc_ref[...], pg_ref[...])


def ffn(x, gain, w_in, w_out, post_gain, *, bm, bf):
    t, d = x.shape
    hidden = w_out.shape[0]
    nf = hidden // bf
    kernel = functools.partial(_ffn_kernel, nf=nf)
    return pl.pallas_call(
        kernel,
        grid=(t // bm, nf),
        in_specs=[
            pl.BlockSpec((bm, d), lambda i, f: (i, 0)),
            pl.BlockSpec((1, d), lambda i, f: (0, 0)),
            pl.BlockSpec((d, bf), lambda i, f: (0, f)),
            pl.BlockSpec((d, bf), lambda i, f: (0, nf + f)),
            pl.BlockSpec((bf, d), lambda i, f: (f, 0)),
            pl.BlockSpec((1, d), lambda i, f: (0, 0)),
        ],
        out_specs=pl.BlockSpec((bm, d), lambda i, f: (i, 0)),
        out_shape=jax.ShapeDtypeStruct((t, d), F32),
        scratch_shapes=[pltpu.VMEM((bm, d), BF16), pltpu.VMEM((bm, d), F32)],
        compiler_params=_cparams("parallel", "arbitrary"),
        name="ffn",
    )(x, gain.reshape(1, d), w_in, w_in, w_out, post_gain.reshape(1, d))


def kernel(x, mem, mix_pre_gain, mix_post_gain, w_in, gm_v_gain, gm_w_s, gm_b_s, s5_lam_re, s5_lam_im, s5_log_dt, s5_b_re, s5_b_im, s5_c_re, s5_c_im, s5_d, s5_w_glu, w_br_a, w_br_b, w_br_c, w_out, xattn_pre_gain, xattn_post_gain, mem_gain, w_xq, w_xkv, w_xo, ffn_pre_gain, ffn_post_gain, w_ffn_in, w_ffn_out):
    batch, seq, d = x.shape
    depth = w_in.shape[0]
    t = batch * seq
    bm = math.gcd(seq, 512)
    xs = x.reshape(t, d)
    mem2 = mem.reshape(batch * mem.shape[1], d)
    n_in = w_in.shape[2]
    bn = n_in // 7 if n_in % (7 * LANES) == 0 else n_in
    for l in range(depth):
        w16 = lambda w: w[l].astype(BF16)
        proj = norm_matmul(xs, mix_pre_gain[l], w16(w_in), bm=bm, bn=bn)
        y_a = sb_attention(proj, batch, seq)
        y_b = spatial_gating(proj, gm_v_gain[l], gm_w_s[l], gm_b_s[l], bm=bm)
        wb, wc, pw = _s5_tables(s5_lam_re[l], s5_lam_im[l], s5_log_dt[l], s5_b_re[l], s5_b_im[l],
                                s5_c_re[l], s5_c_im[l])
        y_c = s5_mixer(proj, batch, seq, wb, wc, pw, s5_d[l], w16(s5_w_glu), ts=math.gcd(seq, 256))
        xs = merge_out(y_a, y_b, y_c, proj, xs, w16(w_br_a), w16(w_br_b), w16(w_br_c), w16(w_out),
                       mix_post_gain[l], bm=bm)
        kv = norm_matmul(mem2, mem_gain[l], w16(w_xkv), bm=math.gcd(mem2.shape[0], 256), bn=2 * X_WIDTH)
        xs = cross_attention(xs, kv, seq, xattn_pre_gain[l], w16(w_xq), w16(w_xo), xattn_post_gain[l], bm=bm)
        xs = ffn(xs, ffn_pre_gain[l], w16(w_ffn_in), w16(w_ffn_out), ffn_post_gain[l], bm=bm, bf=512)
    return xs.reshape(batch, seq, d)
```

```python
import functools
import math

import jax
import jax.numpy as jnp
from jax import lax
from jax.experimental import pallas as pl
from jax.experimental.pallas import tpu as pltpu

F32 = jnp.float32
BF16 = jnp.bfloat16

RMS_EPS = 1e-6

SB_HEADS = 4
SB_HEAD_DIM = 256
SB_WIDTH = SB_HEADS * SB_HEAD_DIM
GM_GROUPS = 4
GM_CHUNK = 128
GM_WIDTH = 512
S5_WIDTH = 512
S5_GROUP_DIM = 16
S5_GROUPS = S5_WIDTH // S5_GROUP_DIM
S5_STATE = 64
S5_LANES = S5_GROUPS * S5_STATE
X_HEADS = 4
X_HEAD_DIM = 128
X_WIDTH = X_HEADS * X_HEAD_DIM

D_MODEL = 2048
N_BRANCH = 3

W_IN_GATE_COL = 3 * SB_WIDTH + 2 * GM_WIDTH + S5_WIDTH
OFF_GATE = 0
OFF_Q = N_BRANCH * D_MODEL
OFF_K = OFF_Q + SB_WIDTH
OFF_V = OFF_K + SB_WIDTH
OFF_GM = OFF_V + SB_WIDTH
OFF_S5 = OFF_GM + 2 * GM_WIDTH

SUBLANES = 8
LANES = 128

EXP_ZERO_BELOW = -104.0

V7X_VMEM_BYTES = 64 * 1024 * 1024
VMEM_LIMIT = V7X_VMEM_BYTES - 4 * 1024 * 1024


def _cparams(*sem):
    return pltpu.CompilerParams(dimension_semantics=sem, vmem_limit_bytes=VMEM_LIMIT)


def _rms(xf, gain):
    ms = jnp.mean(xf * xf, axis=-1, keepdims=True)
    return xf * lax.rsqrt(ms + RMS_EPS) * gain


def _dot(a, b):
    return jnp.dot(a, b, preferred_element_type=F32)


def _dot_nt(a, b):
    return lax.dot_general(a, b, (((1,), (1,)), ((), ())), preferred_element_type=F32)


def _resident(shape):
    return pl.BlockSpec(shape, lambda *_: (0,) * len(shape), pipeline_mode=pl.Buffered(1))


def _norm_matmul_kernel(x_ref, g_ref, w_ref, o_ref, h_ref):
    @pl.when(pl.program_id(1) == 0)
    def _():
        h_ref[...] = _rms(x_ref[...], g_ref[...]).astype(BF16)

    o_ref[...] = _dot(h_ref[...], w_ref[...]).astype(o_ref.dtype)


def norm_matmul(x, gain, w, *, bm, bn):
    t, d = x.shape
    n = w.shape[1]
    return pl.pallas_call(
        _norm_matmul_kernel,
        grid=(t // bm, n // bn),
        in_specs=[
            pl.BlockSpec((bm, d), lambda i, j: (i, 0)),
            pl.BlockSpec((1, d), lambda i, j: (0, 0)),
            pl.BlockSpec((d, bn), lambda i, j: (0, j)),
        ],
        out_specs=pl.BlockSpec((bm, bn), lambda i, j: (i, j)),
        out_shape=jax.ShapeDtypeStruct((t, n), BF16),
        scratch_shapes=[pltpu.VMEM((bm, d), BF16)],
        compiler_params=_cparams("parallel", "arbitrary"),
        name="norm_matmul",
    )(x, gain.reshape(1, d), w)


def _sb_tile(z, mask, carry, v, uo, bk):
    sp = jnp.log(1.0 + jnp.exp(-jnp.abs(z)))
    ls_neg = jnp.minimum(-z, 0.0) - sp
    ls_pos = z + ls_neg
    if mask is not None:
        ls_neg = jnp.where(mask, ls_neg, 0.0)
    hi = ls_neg.astype(BF16)
    lo = (ls_neg - hi.astype(F32)).astype(BF16)
    cs = _dot(hi, uo) + _dot(lo, uo)
    log_w = ls_pos + cs[:, :bk] + jnp.concatenate([carry] * (bk // LANES), axis=1)
    w = jnp.exp(log_w)
    if mask is not None:
        w = jnp.where(mask, w, 0.0)
    return _dot(w.astype(BF16), v), carry + cs[:, bk:]


def _sb_attn_kernel(q_ref, k_ref, v_ref, uo_ref, o_ref, acc_ref, carry_ref, *, bq, bk, scale):
    i = pl.program_id(2)
    uo = uo_ref[...]
    base = i * bq
    acc_ref[...] = jnp.zeros_like(acc_ref)
    carry_ref[...] = jnp.zeros_like(carry_ref)

    for d in reversed(range(bq // bk)):
        rows = slice(d * bk, bq)
        m = bq - d * bk
        off = pl.multiple_of(base + d * bk, bk)
        z = _dot_nt(q_ref[rows, :], k_ref[pl.ds(off, bk), :]) * scale
        row = lax.broadcasted_iota(jnp.int32, (m, bk), 0)
        col = lax.broadcasted_iota(jnp.int32, (m, bk), 1)
        pv, carry = _sb_tile(z, col < row, carry_ref[rows, :], v_ref[pl.ds(off, bk), :], uo, bk)
        acc_ref[rows, :] += pv
        carry_ref[rows, :] = carry

    def cond(state):
        kb, live = state
        return jnp.logical_and(kb >= 0, live > EXP_ZERO_BELOW)

    def body(state):
        kb, _ = state
        off = pl.multiple_of(kb * bk, bk)
        z = _dot_nt(q_ref[...], k_ref[pl.ds(off, bk), :]) * scale
        pv, carry = _sb_tile(z, None, carry_ref[...], v_ref[pl.ds(off, bk), :], uo, bk)
        acc_ref[...] += pv
        carry_ref[...] = carry
        return kb - 1, jnp.max(carry)

    lax.while_loop(cond, body, (i * (bq // bk) - 1, jnp.max(carry_ref[...])))
    o_ref[...] = acc_ref[...].astype(o_ref.dtype)


def sb_attention(proj, batch, seq, *, bq, bk=LANES):
    t = proj.shape[0]
    nq = seq // bq
    dh = SB_HEAD_DIM
    r = lax.broadcasted_iota(jnp.int32, (bk, bk + LANES), 0)
    c = lax.broadcasted_iota(jnp.int32, (bk, bk + LANES), 1)
    uo = jnp.where(jnp.logical_or(r > c, c >= bk), 1.0, 0.0).astype(BF16)
    kernel = functools.partial(_sb_attn_kernel, bq=bq, bk=bk, scale=dh ** -0.5)
    return pl.pallas_call(
        kernel,
        grid=(batch, SB_HEADS, nq),
        in_specs=[
            pl.BlockSpec((bq, dh), lambda b, h, i: (b * nq + i, OFF_Q // dh + h)),
            pl.BlockSpec((seq, dh), lambda b, h, i: (b, OFF_K // dh + h)),
            pl.BlockSpec((seq, dh), lambda b, h, i: (b, OFF_V // dh + h)),
            pl.BlockSpec((bk, bk + LANES), lambda b, h, i: (0, 0)),
        ],
        out_specs=pl.BlockSpec((bq, dh), lambda b, h, i: (b * nq + i, h)),
        out_shape=jax.ShapeDtypeStruct((t, SB_WIDTH), BF16),
        scratch_shapes=[pltpu.VMEM((bq, dh), F32), pltpu.VMEM((bq, LANES), F32)],
        compiler_params=_cparams("parallel", "parallel", "arbitrary"),
        name="sb_attention",
    )(proj, proj, proj, uo)


def _gm_kernel(uv_ref, gain_ref, w_ref, b_ref, o_ref, *, chunks):
    uv = jax.nn.gelu(uv_ref[...].astype(F32))
    u = uv[:, :GM_WIDTH]
    v = _rms(uv[:, GM_WIDTH:], gain_ref[...]).astype(BF16)
    row = lax.broadcasted_iota(jnp.int32, (GM_CHUNK, GM_CHUNK), 0)
    col = lax.broadcasted_iota(jnp.int32, (GM_CHUNK, GM_CHUNK), 1)
    gd = GM_WIDTH // GM_GROUPS
    for g in range(GM_GROUPS):
        wg = jnp.where(col <= row, w_ref[g], 0.0).astype(BF16)
        bg = b_ref[g]
        for c in range(chunks):
            rs = slice(c * GM_CHUNK, (c + 1) * GM_CHUNK)
            cs = slice(g * gd, (g + 1) * gd)
            mixed = _dot(wg, v[rs, cs]) + bg
            o_ref[rs, cs] = (u[rs, cs] * mixed).astype(o_ref.dtype)


def spatial_gating(proj, v_gain, w_s, b_s, *, bm):
    t = proj.shape[0]
    gd = GM_WIDTH // GM_GROUPS
    b_full = jnp.broadcast_to(b_s[:, :, None], (GM_GROUPS, GM_CHUNK, gd)).astype(F32)
    kernel = functools.partial(_gm_kernel, chunks=bm // GM_CHUNK)
    return pl.pallas_call(
        kernel,
        grid=(t // bm,),
        in_specs=[
            pl.BlockSpec((bm, 2 * GM_WIDTH), lambda i: (i, OFF_GM // (2 * GM_WIDTH))),
            pl.BlockSpec((1, GM_WIDTH), lambda i: (0, 0)),
            pl.BlockSpec((GM_GROUPS, GM_CHUNK, GM_CHUNK), lambda i: (0, 0, 0)),
            pl.BlockSpec((GM_GROUPS, GM_CHUNK, gd), lambda i: (0, 0, 0)),
        ],
        out_specs=pl.BlockSpec((bm, GM_WIDTH), lambda i: (i, 0)),
        out_shape=jax.ShapeDtypeStruct((t, GM_WIDTH), BF16),
        compiler_params=_cparams("parallel"),
        name="spatial_gating",
    )(proj, v_gain.reshape(1, GM_WIDTH), w_s, b_full)


def _s5_kernel(u_ref, wb_ref, pw_ref, wc_ref, d_ref, wg_ref, o_ref, x_ref, carry_ref, *, ts):
    n = S5_LANES

    @pl.when(pl.program_id(1) == 0)
    def _():
        carry_ref[...] = jnp.zeros_like(carry_ref)

    u = u_ref[...]
    hw = S5_WIDTH // 2
    for hh in range(2):
        x_ref[:, hh * n:(hh + 1) * n] = _dot(u[:, hh * hw:(hh + 1) * hw], wb_ref[hh])

    def body(r, c):
        off = pl.multiple_of(r * SUBLANES, SUBLANES)
        rows = pl.ds(off, SUBLANES)
        for lc in range(n // LANES):
            hh, k = divmod(lc, n // (2 * LANES))
            re = pl.ds(hh * n + k * LANES, LANES)
            im = pl.ds(hh * n + n // 2 + k * LANES, LANES)
            xr = x_ref[rows, re]
            xi = x_ref[rows, im]
            for step, shift in enumerate((1, 2, 4)):
                pr = pw_ref[step, :, re]
                pi = pw_ref[step, :, im]
                sr = pltpu.roll(xr, shift, 0)
                si = pltpu.roll(xi, shift, 0)
                xr, xi = xr + (pr * sr - pi * si), xi + (pr * si + pi * sr)
            ar = pw_ref[3, :, re]
            ai = pw_ref[3, :, im]
            cr = carry_ref[:, re]
            ci = carry_ref[:, im]
            xr = xr + (ar * cr - ai * ci)
            xi = xi + (ar * ci + ai * cr)
            x_ref[rows, re] = xr
            x_ref[rows, im] = xi
            carry_ref[:, re] = jnp.broadcast_to(xr[SUBLANES - 1:, :], (SUBLANES, LANES))
            carry_ref[:, im] = jnp.broadcast_to(xi[SUBLANES - 1:, :], (SUBLANES, LANES))
        return c

    lax.fori_loop(0, ts // SUBLANES, body, 0)

    y = jnp.concatenate([_dot(x_ref[:, hh * n:(hh + 1) * n].astype(BF16), wc_ref[hh]) for hh in range(2)], axis=-1)
    y = jax.nn.gelu(y + d_ref[...] * u.astype(F32)).astype(BF16)
    ag = _dot(y, wg_ref[...])
    o_ref[...] = (ag[:, :S5_WIDTH] * jax.nn.sigmoid(ag[:, S5_WIDTH:])).astype(o_ref.dtype)


def _s5_tables(lam_re, lam_im, log_dt, b_re, b_im, c_re, c_im):
    g, n, c = S5_GROUPS, S5_STATE, S5_GROUP_DIM
    lr, li = lam_re.astype(F32), lam_im.astype(F32)
    dt = jnp.exp(log_dt.astype(F32))[:, None]
    mag = jnp.exp(lr * dt)
    a_re, a_im = mag * jnp.cos(li * dt), mag * jnp.sin(li * dt)
    den = lr * lr + li * li
    x_ = a_re - 1.0
    f_re = (x_ * lr + a_im * li) / den
    f_im = (a_im * lr - x_ * li) / den
    br, bi = b_re.astype(F32), b_im.astype(F32)
    bb_re = f_re[..., None] * br - f_im[..., None] * bi
    bb_im = f_re[..., None] * bi + f_im[..., None] * br
    gh = g // 2
    eye = jnp.eye(gh, dtype=F32)

    def in_map(bb):
        return jnp.einsum('hgnc,gk->hgckn', bb.reshape(2, gh, n, c), eye).reshape(2, gh * c, gh * n)

    def out_map(cc):
        return jnp.einsum('hgcn,gk->hgnkc', cc.astype(F32).reshape(2, gh, c, n), eye).reshape(2, gh * n, gh * c)

    wb = jnp.concatenate([in_map(bb_re), in_map(bb_im)], axis=2).astype(BF16)
    wc = jnp.concatenate([out_map(c_re), -out_map(c_im)], axis=1).astype(BF16)

    def lanes(pr, pi):
        hl = gh * n
        return jnp.concatenate([pr[..., :hl], pi[..., :hl], pr[..., hl:], pi[..., hl:]], axis=-1)

    ar, ai = a_re.reshape(-1), a_im.reshape(-1)

    def cmul(p, q):
        return p[0] * q[0] - p[1] * q[1], p[0] * q[1] + p[1] * q[0]

    pows = [(ar, ai)]
    for _ in range(SUBLANES - 1):
        pows.append(cmul(pows[-1], (ar, ai)))
    rows = jnp.arange(SUBLANES)[:, None]
    tabs = []
    for shift in (1, 2, 4):
        pr, pi = pows[shift - 1]
        keep = rows >= shift
        tabs.append(lanes(jnp.where(keep, pr[None], 0.0), jnp.where(keep, pi[None], 0.0)))
    tabs.append(lanes(jnp.stack([p[0] for p in pows]), jnp.stack([p[1] for p in pows])))
    return wb, wc, jnp.stack(tabs).astype(F32)


def s5_mixer(proj, batch, seq, wb, wc, pw, d_skip, w_glu, *, ts):
    t = proj.shape[0]
    nt = seq // ts
    n2 = 2 * S5_LANES
    kernel = functools.partial(_s5_kernel, ts=ts)
    return pl.pallas_call(
        kernel,
        grid=(batch, nt),
        in_specs=[
            pl.BlockSpec((ts, S5_WIDTH), lambda b, i: (b * nt + i, OFF_S5 // S5_WIDTH)),
            _resident((2, S5_WIDTH // 2, S5_LANES)),
            _resident((4, SUBLANES, n2)),
            _resident((2, S5_LANES, S5_WIDTH // 2)),
            _resident((1, S5_WIDTH)),
            _resident((S5_WIDTH, 2 * S5_WIDTH)),
        ],
        out_specs=pl.BlockSpec((ts, S5_WIDTH), lambda b, i: (b * nt + i, 0)),
        out_shape=jax.ShapeDtypeStruct((t, S5_WIDTH), BF16),
        scratch_shapes=[pltpu.VMEM((ts, n2), F32), pltpu.VMEM((SUBLANES, n2), F32)],
        compiler_params=_cparams("parallel", "arbitrary"),
        name="s5_mixer",
    )(proj, wb, pw, wc, d_skip.reshape(1, S5_WIDTH).astype(F32), w_glu)


def _merge_kernel(ya_ref, yb_ref, yc_ref, g0_ref, g1_ref, g2_ref, x_ref, wa_ref, wb_ref, wc_ref,
                  wo_ref, gain_ref, o_ref, m_ref, *, cw):
    ya, yb, yc = ya_ref[...], yb_ref[...], yc_ref[...]
    for k in range(m_ref.shape[1] // cw):
        cs = slice(k * cw, (k + 1) * cw)
        m = (jax.nn.sigmoid(g0_ref[:, cs].astype(F32)) * _dot(ya, wa_ref[:, cs])
             + jax.nn.sigmoid(g1_ref[:, cs].astype(F32)) * _dot(yb, wb_ref[:, cs])
             + jax.nn.sigmoid(g2_ref[:, cs].astype(F32)) * _dot(yc, wc_ref[:, cs]))
        m_ref[:, cs] = m.astype(BF16)
    y = _dot(m_ref[...], wo_ref[...])
    o_ref[...] = x_ref[...] + _rms(y, gain_ref[...])


def merge_out(y_a, y_b, y_c, proj, x, w_a, w_b, w_c, w_o, gain, *, bm, cw=512):
    t, d = x.shape
    g0 = OFF_GATE // d
    kernel = functools.partial(_merge_kernel, cw=cw)
    return pl.pallas_call(
        kernel,
        grid=(t // bm,),
        in_specs=[
            pl.BlockSpec((bm, SB_WIDTH), lambda i: (i, 0)),
            pl.BlockSpec((bm, GM_WIDTH), lambda i: (i, 0)),
            pl.BlockSpec((bm, S5_WIDTH), lambda i: (i, 0)),
            pl.BlockSpec((bm, d), lambda i: (i, g0)),
            pl.BlockSpec((bm, d), lambda i: (i, g0 + 1)),
            pl.BlockSpec((bm, d), lambda i: (i, g0 + 2)),
            pl.BlockSpec((bm, d), lambda i: (i, 0)),
            _resident((SB_WIDTH, d)),
            _resident((GM_WIDTH, d)),
            _resident((S5_WIDTH, d)),
            _resident((d, d)),
            _resident((1, d)),
        ],
        out_specs=pl.BlockSpec((bm, d), lambda i: (i, 0)),
        out_shape=jax.ShapeDtypeStruct((t, d), F32),
        scratch_shapes=[pltpu.VMEM((bm, d), BF16)],
        compiler_params=_cparams("parallel"),
        name="merge_out",
    )(y_a, y_b, y_c, proj, proj, proj, x, w_a, w_b, w_c, w_o, gain.reshape(1, d))


def _xattn_kernel(x_ref, g_ref, wq_ref, kv_ref, wo_ref, pg_ref, o_ref, *, scale):
    x = x_ref[...]
    h = _rms(x, g_ref[...]).astype(BF16)
    q = (_dot(h, wq_ref[...]) * scale).astype(BF16)
    outs = []
    for hd in range(X_HEADS):
        cs = slice(hd * X_HEAD_DIM, (hd + 1) * X_HEAD_DIM)
        vs = slice(X_WIDTH + hd * X_HEAD_DIM, X_WIDTH + (hd + 1) * X_HEAD_DIM)
        z = _dot_nt(q[:, cs], kv_ref[:, cs])
        e = jnp.exp(z - jnp.max(z, axis=-1, keepdims=True))
        p = e / jnp.sum(e, axis=-1, keepdims=True)
        outs.append(_dot(p.astype(BF16), kv_ref[:, vs]))
    o = jnp.concatenate(outs, axis=-1).astype(BF16)
    o_ref[...] = x + _rms(_dot(o, wo_ref[...]), pg_ref[...])


def cross_attention(x, kv, seq, gain, w_q, w_o, post_gain, *, bm):
    t, d = x.shape
    mem_len = kv.shape[0] // (t // seq)
    per_batch = seq // bm
    kernel = functools.partial(_xattn_kernel, scale=X_HEAD_DIM ** -0.5)
    return pl.pallas_call(
        kernel,
        grid=(t // bm,),
        in_specs=[
            pl.BlockSpec((bm, d), lambda i: (i, 0)),
            pl.BlockSpec((1, d), lambda i: (0, 0)),
            pl.BlockSpec((d, X_WIDTH), lambda i: (0, 0)),
            pl.BlockSpec((mem_len, 2 * X_WIDTH), lambda i: (i // per_batch, 0)),
            pl.BlockSpec((X_WIDTH, d), lambda i: (0, 0)),
            pl.BlockSpec((1, d), lambda i: (0, 0)),
        ],
        out_specs=pl.BlockSpec((bm, d), lambda i: (i, 0)),
        out_shape=jax.ShapeDtypeStruct((t, d), F32),
        compiler_params=_cparams("parallel"),
        name="cross_attention",
    )(x, gain.reshape(1, d), w_q, kv, w_o, post_gain.reshape(1, d))


def _ffn_kernel(x_ref, g_ref, wg_ref, wu_ref, wo_ref, pg_ref, o_ref, h_ref, *, nf):
    f = pl.program_id(1)

    @pl.when(f == 0)
    def _():
        h_ref[...] = _rms(x_ref[...], g_ref[...]).astype(BF16)
        o_ref[...] = jnp.zeros_like(o_ref)

    h = h_ref[...]
    a = (jax.nn.silu(_dot(h, wg_ref[...])) * _dot(h, wu_ref[...])).astype(BF16)
    o_ref[...] += _dot(a, wo_ref[...])

    @pl.when(f == nf - 1)
    def _():
        o_ref[...] = x_ref[...] + _rms(o_ref[...], pg_ref[...])


def ffn(x, gain, w_in, w_out, post_gain, *, bm, bf):
    t, d = x.shape
    hidden = w_out.shape[0]
    nf = hidden // bf
    kernel = functools.partial(_ffn_kernel, nf=nf)
    return pl.pallas_call(
        kernel,
        grid=(t // bm, nf),
        in_specs=[
            pl.BlockSpec((bm, d), lambda i, f: (i, 0), pipeline_mode=pl.Buffered(1)),
            pl.BlockSpec((1, d), lambda i, f: (0, 0)),
            pl.BlockSpec((d, bf), lambda i, f: (0, f)),
            pl.BlockSpec((d, bf), lambda i, f: (0, nf + f)),
            pl.BlockSpec((bf, d), lambda i, f: (f, 0)),
            pl.BlockSpec((1, d), lambda i, f: (0, 0)),
        ],
        out_specs=pl.BlockSpec((bm, d), lambda i, f: (i, 0)),
        out_shape=jax.ShapeDtypeStruct((t, d), F32),
        scratch_shapes=[pltpu.VMEM((bm, d), BF16)],
        compiler_params=_cparams("parallel", "arbitrary"),
        name="ffn",
    )(x, gain.reshape(1, d), w_in, w_in, w_out, post_gain.reshape(1, d))


def kernel(x, mem, mix_pre_gain, mix_post_gain, w_in, gm_v_gain, gm_w_s, gm_b_s, s5_lam_re, s5_lam_im, s5_log_dt, s5_b_re, s5_b_im, s5_c_re, s5_c_im, s5_d, s5_w_glu, w_br_a, w_br_b, w_br_c, w_out, xattn_pre_gain, xattn_post_gain, mem_gain, w_xq, w_xkv, w_xo, ffn_pre_gain, ffn_post_gain, w_ffn_in, w_ffn_out):
    batch, seq, d = x.shape
    depth = w_in.shape[0]
    t = batch * seq
    bm = math.gcd(seq, 512)
    xs = x.reshape(t, d)
    mem2 = mem.reshape(batch * mem.shape[1], d)
    n_in = w_in.shape[2]
    bn = n_in // 3 if n_in % (3 * LANES) == 0 else n_in
    assert d == D_MODEL and n_in == W_IN_GATE_COL + N_BRANCH * D_MODEL
    for l in range(depth):
        w16 = lambda w: w[l].astype(BF16)
        w_in_l = jnp.concatenate([w_in[l][:, W_IN_GATE_COL:], w_in[l][:, :W_IN_GATE_COL]], axis=1).astype(BF16)
        proj = norm_matmul(xs, mix_pre_gain[l], w_in_l, bm=bm, bn=bn)
        y_a = sb_attention(proj, batch, seq, bq=math.gcd(seq, 512), bk=256)
        y_b = spatial_gating(proj, gm_v_gain[l], gm_w_s[l], gm_b_s[l], bm=bm)
        wb, wc, pw = _s5_tables(s5_lam_re[l], s5_lam_im[l], s5_log_dt[l], s5_b_re[l], s5_b_im[l],
                                s5_c_re[l], s5_c_im[l])
        y_c = s5_mixer(proj, batch, seq, wb, wc, pw, s5_d[l], w16(s5_w_glu), ts=math.gcd(seq, 256))
        xs = merge_out(y_a, y_b, y_c, proj, xs, w16(w_br_a), w16(w_br_b), w16(w_br_c), w16(w_out),
                       mix_post_gain[l], bm=bm)
        kv = norm_matmul(mem2, mem_gain[l], w16(w_xkv), bm=math.gcd(mem2.shape[0], 256), bn=2 * X_WIDTH)
        xs = cross_attention(xs, kv, seq, xattn_pre_gain[l], w16(w_xq), w16(w_xo), xattn_post_gain[l], bm=bm)
        xs = ffn(xs, ffn_pre_gain[l], w16(w_ffn_in), w16(w_ffn_out), ffn_post_gain[l],
                 bm=math.gcd(seq, 1024), bf=512)
    return xs.reshape(batch, seq, d)
```

```python
import functools
import math

import jax
import jax.numpy as jnp
from jax import lax
from jax.experimental import pallas as pl
from jax.experimental.pallas import tpu as pltpu

F32 = jnp.float32
BF16 = jnp.bfloat16

RMS_EPS = 1e-6

SB_HEADS = 4
SB_HEAD_DIM = 256
SB_WIDTH = SB_HEADS * SB_HEAD_DIM
GM_GROUPS = 4
GM_CHUNK = 128
GM_WIDTH = 512
S5_WIDTH = 512
S5_GROUP_DIM = 16
S5_GROUPS = S5_WIDTH // S5_GROUP_DIM
S5_STATE = 64
S5_LANES = S5_GROUPS * S5_STATE
X_HEADS = 4
X_HEAD_DIM = 128
X_WIDTH = X_HEADS * X_HEAD_DIM

D_MODEL = 2048
N_BRANCH = 3

W_IN_GATE_COL = 3 * SB_WIDTH + 2 * GM_WIDTH + S5_WIDTH
OFF_GATE = 0
OFF_Q = N_BRANCH * D_MODEL
OFF_K = OFF_Q + SB_WIDTH
OFF_V = OFF_K + SB_WIDTH
OFF_GM = OFF_V + SB_WIDTH
OFF_S5 = OFF_GM + 2 * GM_WIDTH

SUBLANES = 8
LANES = 128

EXP_ZERO_BELOW = -104.0
SB_DEAD_CARRY = -1e30

V7X_VMEM_BYTES = 64 * 1024 * 1024
VMEM_LIMIT = V7X_VMEM_BYTES - 4 * 1024 * 1024


def _cparams(*sem):
    return pltpu.CompilerParams(dimension_semantics=sem, vmem_limit_bytes=VMEM_LIMIT)


def _rms(xf, gain):
    ms = jnp.mean(xf * xf, axis=-1, keepdims=True)
    return xf * lax.rsqrt(ms + RMS_EPS) * gain


def _dot(a, b):
    return jnp.dot(a, b, preferred_element_type=F32)


def _dot_nt(a, b):
    return lax.dot_general(a, b, (((1,), (1,)), ((), ())), preferred_element_type=F32)


def _resident(shape):
    return pl.BlockSpec(shape, lambda *_: (0,) * len(shape), pipeline_mode=pl.Buffered(1))


def _norm_matmul_kernel(x_ref, g_ref, w_ref, o_ref, h_ref):
    @pl.when(pl.program_id(1) == 0)
    def _():
        h_ref[...] = _rms(x_ref[...], g_ref[...]).astype(BF16)

    o_ref[...] = _dot(h_ref[...], w_ref[...]).astype(o_ref.dtype)


def norm_matmul(x, gain, w, *, bm, bn):
    t, d = x.shape
    n = w.shape[1]
    return pl.pallas_call(
        _norm_matmul_kernel,
        grid=(t // bm, n // bn),
        in_specs=[
            pl.BlockSpec((bm, d), lambda i, j: (i, 0)),
            pl.BlockSpec((1, d), lambda i, j: (0, 0)),
            pl.BlockSpec((d, bn), lambda i, j: (0, j)),
        ],
        out_specs=pl.BlockSpec((bm, bn), lambda i, j: (i, j)),
        out_shape=jax.ShapeDtypeStruct((t, n), BF16),
        scratch_shapes=[pltpu.VMEM((bm, d), BF16)],
        compiler_params=_cparams("parallel", "arbitrary"),
        name="norm_matmul",
    )(x, gain.reshape(1, d), w)


def _sb_tile(z, mask, carry, v, uo, bk):
    sp = jnp.log(1.0 + jnp.exp(-jnp.abs(z)))
    ls_neg = jnp.minimum(-z, 0.0) - sp
    ls_pos = z + ls_neg
    if mask is not None:
        ls_neg = jnp.where(mask, ls_neg, 0.0)
    hi = ls_neg.astype(BF16)
    lo = (ls_neg - hi.astype(F32)).astype(BF16)
    cs = _dot(hi, uo) + _dot(lo, uo)
    log_w = ls_pos + cs[:, :bk] + jnp.concatenate([carry] * (bk // LANES), axis=1)
    w = jnp.exp(log_w)
    if mask is not None:
        w = jnp.where(mask, w, 0.0)
    return _dot(w.astype(BF16), v), carry + cs[:, bk:]


def _sb_attn_kernel(q_ref, k_ref, v_ref, uo_ref, o_ref, acc_ref, carry_ref, *, bq, bk, chains, scale):
    i = pl.program_id(2)
    uo = uo_ref[...]
    nd = bq // bk
    acc_ref[...] = jnp.zeros_like(acc_ref)
    carry_ref[...] = jnp.zeros_like(carry_ref)

    for c in range(chains):
        for d in reversed(range(nd)):
            rows = slice(c * bq + d * bk, (c + 1) * bq)
            m = bq - d * bk
            off = pl.multiple_of((i * chains + c) * bq + d * bk, bk)
            z = _dot_nt(q_ref[rows, :], k_ref[pl.ds(off, bk), :]) * scale
            row = lax.broadcasted_iota(jnp.int32, (m, bk), 0)
            col = lax.broadcasted_iota(jnp.int32, (m, bk), 1)
            pv, carry = _sb_tile(z, col < row, carry_ref[rows, :], v_ref[pl.ds(off, bk), :], uo, bk)
            acc_ref[rows, :] += pv
            carry_ref[rows, :] = carry

    first = [(i * chains + c) * nd - 1 for c in range(chains)]

    def cond(state):
        t, lives = state
        live = [jnp.logical_and(first[c] - t >= 0, lives[c] > EXP_ZERO_BELOW) for c in range(chains)]
        return functools.reduce(jnp.logical_or, live)

    def body(state):
        t, _ = state
        pvs, lives = [], []
        for c in range(chains):
            rows = slice(c * bq, (c + 1) * bq)
            kb = first[c] - t
            off = pl.multiple_of(jnp.maximum(kb, 0) * bk, bk)
            carry_in = carry_ref[rows, :] + jnp.where(kb < 0, SB_DEAD_CARRY, 0.0)
            z = _dot_nt(q_ref[rows, :], k_ref[pl.ds(off, bk), :]) * scale
            pv, carry = _sb_tile(z, None, carry_in, v_ref[pl.ds(off, bk), :], uo, bk)
            carry_ref[rows, :] = carry
            pvs.append(pv)
            lives.append(jnp.max(carry))
        acc_ref[...] += jnp.concatenate(pvs, axis=0)
        return t + 1, tuple(lives)

    lax.while_loop(cond, body, body((jnp.int32(0), None)))
    o_ref[...] = acc_ref[...].astype(o_ref.dtype)


def sb_attention(proj, batch, seq, *, bq, bk=LANES, chains=1):
    t = proj.shape[0]
    nq = seq // (bq * chains)
    bq_all = bq * chains
    dh = SB_HEAD_DIM
    r = lax.broadcasted_iota(jnp.int32, (bk, bk + LANES), 0)
    c = lax.broadcasted_iota(jnp.int32, (bk, bk + LANES), 1)
    uo = jnp.where(jnp.logical_or(r > c, c >= bk), 1.0, 0.0).astype(BF16)
    kernel = functools.partial(_sb_attn_kernel, bq=bq, bk=bk, chains=chains, scale=dh ** -0.5)
    return pl.pallas_call(
        kernel,
        grid=(batch, SB_HEADS, nq),
        in_specs=[
            pl.BlockSpec((bq_all, dh), lambda b, h, i: (b * nq + i, OFF_Q // dh + h)),
            pl.BlockSpec((seq, dh), lambda b, h, i: (b, OFF_K // dh + h)),
            pl.BlockSpec((seq, dh), lambda b, h, i: (b, OFF_V // dh + h)),
            pl.BlockSpec((bk, bk + LANES), lambda b, h, i: (0, 0)),
        ],
        out_specs=pl.BlockSpec((bq_all, dh), lambda b, h, i: (b * nq + i, h)),
        out_shape=jax.ShapeDtypeStruct((t, SB_WIDTH), BF16),
        scratch_shapes=[pltpu.VMEM((bq_all, dh), F32), pltpu.VMEM((bq_all, LANES), F32)],
        compiler_params=_cparams("parallel", "parallel", "arbitrary"),
        name="sb_attention",
    )(proj, proj, proj, uo)


def _gm_kernel(uv_ref, gain_ref, w_ref, b_ref, o_ref, *, chunks):
    uv = jax.nn.gelu(uv_ref[...].astype(F32))
    u = uv[:, :GM_WIDTH]
    v = _rms(uv[:, GM_WIDTH:], gain_ref[...]).astype(BF16)
    row = lax.broadcasted_iota(jnp.int32, (GM_CHUNK, GM_CHUNK), 0)
    col = lax.broadcasted_iota(jnp.int32, (GM_CHUNK, GM_CHUNK), 1)
    gd = GM_WIDTH // GM_GROUPS
    for g in range(GM_GROUPS):
        wg = jnp.where(col <= row, w_ref[g], 0.0).astype(BF16)
        bg = b_ref[g]
        for c in range(chunks):
            rs = slice(c * GM_CHUNK, (c + 1) * GM_CHUNK)
            cs = slice(g * gd, (g + 1) * gd)
            mixed = _dot(wg, v[rs, cs]) + bg
            o_ref[rs, cs] = (u[rs, cs] * mixed).astype(o_ref.dtype)


def spatial_gating(proj, v_gain, w_s, b_s, *, bm):
    t = proj.shape[0]
    gd = GM_WIDTH // GM_GROUPS
    b_full = jnp.broadcast_to(b_s[:, :, None], (GM_GROUPS, GM_CHUNK, gd)).astype(F32)
    kernel = functools.partial(_gm_kernel, chunks=bm // GM_CHUNK)
    return pl.pallas_call(
        kernel,
        grid=(t // bm,),
        in_specs=[
            pl.BlockSpec((bm, 2 * GM_WIDTH), lambda i: (i, OFF_GM // (2 * GM_WIDTH))),
            pl.BlockSpec((1, GM_WIDTH), lambda i: (0, 0)),
            pl.BlockSpec((GM_GROUPS, GM_CHUNK, GM_CHUNK), lambda i: (0, 0, 0)),
            pl.BlockSpec((GM_GROUPS, GM_CHUNK, gd), lambda i: (0, 0, 0)),
        ],
        out_specs=pl.BlockSpec((bm, GM_WIDTH), lambda i: (i, 0)),
        out_shape=jax.ShapeDtypeStruct((t, GM_WIDTH), BF16),
        compiler_params=_cparams("parallel"),
        name="spatial_gating",
    )(proj, v_gain.reshape(1, GM_WIDTH), w_s, b_full)


def _s5_kernel(u_ref, wb_ref, pw_ref, wc_ref, d_ref, wg_ref, o_ref, x_ref, carry_ref, *, ts):
    n = S5_LANES

    @pl.when(pl.program_id(1) == 0)
    def _():
        carry_ref[...] = jnp.zeros_like(carry_ref)

    u = u_ref[...]
    hw = S5_WIDTH // 2
    for hh in range(2):
        x_ref[:, hh * n:(hh + 1) * n] = _dot(u[:, hh * hw:(hh + 1) * hw], wb_ref[hh])

    def body(r, c):
        off = pl.multiple_of(r * SUBLANES, SUBLANES)
        rows = pl.ds(off, SUBLANES)
        for lc in range(n // LANES):
            hh, k = divmod(lc, n // (2 * LANES))
            re = pl.ds(hh * n + k * LANES, LANES)
            im = pl.ds(hh * n + n // 2 + k * LANES, LANES)
            xr = x_ref[rows, re]
            xi = x_ref[rows, im]
            for step, shift in enumerate((1, 2, 4)):
                pr = pw_ref[step, :, re]
                pi = pw_ref[step, :, im]
                sr = pltpu.roll(xr, shift, 0)
                si = pltpu.roll(xi, shift, 0)
                xr, xi = xr + (pr * sr - pi * si), xi + (pr * si + pi * sr)
            ar = pw_ref[3, :, re]
            ai = pw_ref[3, :, im]
            cr = carry_ref[:, re]
            ci = carry_ref[:, im]
            xr = xr + (ar * cr - ai * ci)
            xi = xi + (ar * ci + ai * cr)
            x_ref[rows, re] = xr
            x_ref[rows, im] = xi
            carry_ref[:, re] = jnp.broadcast_to(xr[SUBLANES - 1:, :], (SUBLANES, LANES))
            carry_ref[:, im] = jnp.broadcast_to(xi[SUBLANES - 1:, :], (SUBLANES, LANES))
        return c

    lax.fori_loop(0, ts // SUBLANES, body, 0)

    y = jnp.concatenate([_dot(x_ref[:, hh * n:(hh + 1) * n].astype(BF16), wc_ref[hh]) for hh in range(2)], axis=-1)
    y = jax.nn.gelu(y + d_ref[...] * u.astype(F32)).astype(BF16)
    ag = _dot(y, wg_ref[...])
    o_ref[...] = (ag[:, :S5_WIDTH] * jax.nn.sigmoid(ag[:, S5_WIDTH:])).astype(o_ref.dtype)


def _s5_tables(lam_re, lam_im, log_dt, b_re, b_im, c_re, c_im):
    g, n, c = S5_GROUPS, S5_STATE, S5_GROUP_DIM
    lr, li = lam_re.astype(F32), lam_im.astype(F32)
    dt = jnp.exp(log_dt.astype(F32))[:, None]
    mag = jnp.exp(lr * dt)
    a_re, a_im = mag * jnp.cos(li * dt), mag * jnp.sin(li * dt)
    den = lr * lr + li * li
    x_ = a_re - 1.0
    f_re = (x_ * lr + a_im * li) / den
    f_im = (a_im * lr - x_ * li) / den
    br, bi = b_re.astype(F32), b_im.astype(F32)
    bb_re = f_re[..., None] * br - f_im[..., None] * bi
    bb_im = f_re[..., None] * bi + f_im[..., None] * br
    gh = g // 2
    eye = jnp.eye(gh, dtype=F32)

    def in_map(bb):
        return jnp.einsum('hgnc,gk->hgckn', bb.reshape(2, gh, n, c), eye).reshape(2, gh * c, gh * n)

    def out_map(cc):
        return jnp.einsum('hgcn,gk->hgnkc', cc.astype(F32).reshape(2, gh, c, n), eye).reshape(2, gh * n, gh * c)

    wb = jnp.concatenate([in_map(bb_re), in_map(bb_im)], axis=2).astype(BF16)
    wc = jnp.concatenate([out_map(c_re), -out_map(c_im)], axis=1).astype(BF16)

    def lanes(pr, pi):
        hl = gh * n
        return jnp.concatenate([pr[..., :hl], pi[..., :hl], pr[..., hl:], pi[..., hl:]], axis=-1)

    ar, ai = a_re.reshape(-1), a_im.reshape(-1)

    def cmul(p, q):
        return p[0] * q[0] - p[1] * q[1], p[0] * q[1] + p[1] * q[0]

    pows = [(ar, ai)]
    for _ in range(SUBLANES - 1):
        pows.append(cmul(pows[-1], (ar, ai)))
    rows = jnp.arange(SUBLANES)[:, None]
    tabs = []
    for shift in (1, 2, 4):
        pr, pi = pows[shift - 1]
        keep = rows >= shift
        tabs.append(lanes(jnp.where(keep, pr[None], 0.0), jnp.where(keep, pi[None], 0.0)))
    tabs.append(lanes(jnp.stack([p[0] for p in pows]), jnp.stack([p[1] for p in pows])))
    return wb, wc, jnp.stack(tabs).astype(F32)


def s5_mixer(proj, batch, seq, wb, wc, pw, d_skip, w_glu, *, ts):
    t = proj.shape[0]
    nt = seq // ts
    n2 = 2 * S5_LANES
    kernel = functools.partial(_s5_kernel, ts=ts)
    return pl.pallas_call(
        kernel,
        grid=(batch, nt),
        in_specs=[
            pl.BlockSpec((ts, S5_WIDTH), lambda b, i: (b * nt + i, OFF_S5 // S5_WIDTH)),
            _resident((2, S5_WIDTH // 2, S5_LANES)),
            _resident((4, SUBLANES, n2)),
            _resident((2, S5_LANES, S5_WIDTH // 2)),
            _resident((1, S5_WIDTH)),
            _resident((S5_WIDTH, 2 * S5_WIDTH)),
        ],
        out_specs=pl.BlockSpec((ts, S5_WIDTH), lambda b, i: (b * nt + i, 0)),
        out_shape=jax.ShapeDtypeStruct((t, S5_WIDTH), BF16),
        scratch_shapes=[pltpu.VMEM((ts, n2), F32), pltpu.VMEM((SUBLANES, n2), F32)],
        compiler_params=_cparams("parallel", "arbitrary"),
        name="s5_mixer",
    )(proj, wb, pw, wc, d_skip.reshape(1, S5_WIDTH).astype(F32), w_glu)


def _merge_kernel(ya_ref, yb_ref, yc_ref, g0_ref, g1_ref, g2_ref, x_ref, wa_ref, wb_ref, wc_ref,
                  wo_ref, gain_ref, o_ref, m_ref, *, cw):
    ya, yb, yc = ya_ref[...], yb_ref[...], yc_ref[...]
    for k in range(m_ref.shape[1] // cw):
        cs = slice(k * cw, (k + 1) * cw)
        m = (jax.nn.sigmoid(g0_ref[:, cs].astype(F32)) * _dot(ya, wa_ref[:, cs])
             + jax.nn.sigmoid(g1_ref[:, cs].astype(F32)) * _dot(yb, wb_ref[:, cs])
             + jax.nn.sigmoid(g2_ref[:, cs].astype(F32)) * _dot(yc, wc_ref[:, cs]))
        m_ref[:, cs] = m.astype(BF16)
    y = _dot(m_ref[...], wo_ref[...])
    o_ref[...] = x_ref[...] + _rms(y, gain_ref[...])


def merge_out(y_a, y_b, y_c, proj, x, w_a, w_b, w_c, w_o, gain, *, bm, cw=512):
    t, d = x.shape
    g0 = OFF_GATE // d
    kernel = functools.partial(_merge_kernel, cw=cw)
    return pl.pallas_call(
        kernel,
        grid=(t // bm,),
        in_specs=[
            pl.BlockSpec((bm, SB_WIDTH), lambda i: (i, 0)),
            pl.BlockSpec((bm, GM_WIDTH), lambda i: (i, 0)),
            pl.BlockSpec((bm, S5_WIDTH), lambda i: (i, 0)),
            pl.BlockSpec((bm, d), lambda i: (i, g0)),
            pl.BlockSpec((bm, d), lambda i: (i, g0 + 1)),
            pl.BlockSpec((bm, d), lambda i: (i, g0 + 2)),
            pl.BlockSpec((bm, d), lambda i: (i, 0)),
            _resident((SB_WIDTH, d)),
            _resident((GM_WIDTH, d)),
            _resident((S5_WIDTH, d)),
            _resident((d, d)),
            _resident((1, d)),
        ],
        out_specs=pl.BlockSpec((bm, d), lambda i: (i, 0)),
        out_shape=jax.ShapeDtypeStruct((t, d), F32),
        scratch_shapes=[pltpu.VMEM((bm, d), BF16)],
        compiler_params=_cparams("parallel"),
        name="merge_out",
    )(y_a, y_b, y_c, proj, proj, proj, x, w_a, w_b, w_c, w_o, gain.reshape(1, d))


def _xattn_kernel(x_ref, g_ref, wq_ref, kv_ref, wo_ref, pg_ref, o_ref, *, scale):
    x = x_ref[...]
    h = _rms(x, g_ref[...]).astype(BF16)
    q = (_dot(h, wq_ref[...]) * scale).astype(BF16)
    outs = []
    for hd in range(X_HEADS):
        cs = slice(hd * X_HEAD_DIM, (hd + 1) * X_HEAD_DIM)
        vs = slice(X_WIDTH + hd * X_HEAD_DIM, X_WIDTH + (hd + 1) * X_HEAD_DIM)
        z = _dot_nt(q[:, cs], kv_ref[:, cs])
        e = jnp.exp(z - jnp.max(z, axis=-1, keepdims=True))
        p = e / jnp.sum(e, axis=-1, keepdims=True)
        outs.append(_dot(p.astype(BF16), kv_ref[:, vs]))
    o = jnp.concatenate(outs, axis=-1).astype(BF16)
    o_ref[...] = x + _rms(_dot(o, wo_ref[...]), pg_ref[...])


def cross_attention(x, kv, seq, gain, w_q, w_o, post_gain, *, bm):
    t, d = x.shape
    mem_len = kv.shape[0] // (t // seq)
    per_batch = seq // bm
    kernel = functools.partial(_xattn_kernel, scale=X_HEAD_DIM ** -0.5)
    return pl.pallas_call(
        kernel,
        grid=(t // bm,),
        in_specs=[
            pl.BlockSpec((bm, d), lambda i: (i, 0)),
            pl.BlockSpec((1, d), lambda i: (0, 0)),
            pl.BlockSpec((d, X_WIDTH), lambda i: (0, 0)),
            pl.BlockSpec((mem_len, 2 * X_WIDTH), lambda i: (i // per_batch, 0)),
            pl.BlockSpec((X_WIDTH, d), lambda i: (0, 0)),
            pl.BlockSpec((1, d), lambda i: (0, 0)),
        ],
        out_specs=pl.BlockSpec((bm, d), lambda i: (i, 0)),
        out_shape=jax.ShapeDtypeStruct((t, d), F32),
        compiler_params=_cparams("parallel"),
        name="cross_attention",
    )(x, gain.reshape(1, d), w_q, kv, w_o, post_gain.reshape(1, d))


def _ffn_kernel(x_ref, g_ref, wi_ref, wo_ref, pg_ref, o_ref, h_ref, *, nf):
    f = pl.program_id(1)

    @pl.when(f == 0)
    def _():
        h_ref[...] = _rms(x_ref[...], g_ref[...]).astype(BF16)
        o_ref[...] = jnp.zeros_like(o_ref)

    gu = _dot(h_ref[...], wi_ref[...])
    a = jnp.concatenate(
        [jax.nn.silu(gu[:, 2 * k * LANES:(2 * k + 1) * LANES]) * gu[:, (2 * k + 1) * LANES:(2 * k + 2) * LANES]
         for k in range(gu.shape[1] // (2 * LANES))], axis=1).astype(BF16)
    o_ref[...] += _dot(a, wo_ref[...])

    @pl.when(f == nf - 1)
    def _():
        o_ref[...] = x_ref[...] + _rms(o_ref[...], pg_ref[...])


def _interleave_gate_up(w_in):
    d, h2 = w_in.shape
    return w_in.reshape(d, 2, h2 // (2 * LANES), LANES).transpose(0, 2, 1, 3).reshape(d, h2)


def ffn(x, gain, w_in, w_out, post_gain, *, bm, bf):
    t, d = x.shape
    hidden = w_out.shape[0]
    nf = hidden // bf
    kernel = functools.partial(_ffn_kernel, nf=nf)
    return pl.pallas_call(
        kernel,
        grid=(t // bm, nf),
        in_specs=[
            pl.BlockSpec((bm, d), lambda i, f: (i, 0)),
            pl.BlockSpec((1, d), lambda i, f: (0, 0)),
            pl.BlockSpec((d, 2 * bf), lambda i, f: (0, f)),
            pl.BlockSpec((bf, d), lambda i, f: (f, 0)),
            pl.BlockSpec((1, d), lambda i, f: (0, 0)),
        ],
        out_specs=pl.BlockSpec((bm, d), lambda i, f: (i, 0)),
        out_shape=jax.ShapeDtypeStruct((t, d), F32),
        scratch_shapes=[pltpu.VMEM((bm, d), BF16)],
        compiler_params=_cparams("parallel", "arbitrary"),
        name="ffn",
    )(x, gain.reshape(1, d), w_in, w_out, post_gain.reshape(1, d))


def kernel(x, mem, mix_pre_gain, mix_post_gain, w_in, gm_v_gain, gm_w_s, gm_b_s, s5_lam_re, s5_lam_im, s5_log_dt, s5_b_re, s5_b_im, s5_c_re, s5_c_im, s5_d, s5_w_glu, w_br_a, w_br_b, w_br_c, w_out, xattn_pre_gain, xattn_post_gain, mem_gain, w_xq, w_xkv, w_xo, ffn_pre_gain, ffn_post_gain, w_ffn_in, w_ffn_out):
    batch, seq, d = x.shape
    depth = w_in.shape[0]
    t = batch * seq
    bm = math.gcd(seq, 512)
    xs = x.reshape(t, d)
    mem2 = mem.reshape(batch * mem.shape[1], d)
    n_in = w_in.shape[2]
    bn = n_in // 3 if n_in % (3 * LANES) == 0 else n_in
    assert d == D_MODEL and n_in == W_IN_GATE_COL + N_BRANCH * D_MODEL
    for l in range(depth):
        w16 = lambda w: w[l].astype(BF16)
        w_in_l = jnp.concatenate([w_in[l][:, W_IN_GATE_COL:], w_in[l][:, :W_IN_GATE_COL]], axis=1).astype(BF16)
        proj = norm_matmul(xs, mix_pre_gain[l], w_in_l, bm=bm, bn=bn)
        y_a = sb_attention(proj, batch, seq, bq=math.gcd(seq, 256), bk=256, chains=4)
        y_b = spatial_gating(proj, gm_v_gain[l], gm_w_s[l], gm_b_s[l], bm=bm)
        wb, wc, pw = _s5_tables(s5_lam_re[l], s5_lam_im[l], s5_log_dt[l], s5_b_re[l], s5_b_im[l],
                                s5_c_re[l], s5_c_im[l])
        y_c = s5_mixer(proj, batch, seq, wb, wc, pw, s5_d[l], w16(s5_w_glu), ts=math.gcd(seq, 256))
        xs = merge_out(y_a, y_b, y_c, proj, xs, w16(w_br_a), w16(w_br_b), w16(w_br_c), w16(w_out),
                       mix_post_gain[l], bm=bm)
        kv = norm_matmul(mem2, mem_gain[l], w16(w_xkv), bm=math.gcd(mem2.shape[0], 256), bn=2 * X_WIDTH)
        xs = cross_attention(xs, kv, seq, xattn_pre_gain[l], w16(w_xq), w16(w_xo), xattn_post_gain[l], bm=bm)
        xs = ffn(xs, ffn_pre_gain[l], _interleave_gate_up(w_ffn_in[l]).astype(BF16), w16(w_ffn_out),
                 ffn_post_gain[l], bm=bm, bf=512)
    return xs.reshape(batch, seq, d)
```

```python
import functools
import math

import jax
import jax.numpy as jnp
from jax import lax
from jax.experimental import pallas as pl
from jax.experimental.pallas import tpu as pltpu

F32 = jnp.float32
BF16 = jnp.bfloat16

RMS_EPS = 1e-6

SB_HEADS = 4
SB_HEAD_DIM = 256
SB_WIDTH = SB_HEADS * SB_HEAD_DIM
GM_GROUPS = 4
GM_CHUNK = 128
GM_WIDTH = 512
S5_WIDTH = 512
S5_GROUP_DIM = 16
S5_GROUPS = S5_WIDTH // S5_GROUP_DIM
S5_STATE = 64
S5_LANES = S5_GROUPS * S5_STATE
X_HEADS = 4
X_HEAD_DIM = 128
X_WIDTH = X_HEADS * X_HEAD_DIM

D_MODEL = 2048
N_BRANCH = 3

W_IN_GATE_COL = 3 * SB_WIDTH + 2 * GM_WIDTH + S5_WIDTH
OFF_GATE = 0
OFF_Q = N_BRANCH * D_MODEL
OFF_K = OFF_Q + SB_WIDTH
OFF_V = OFF_K + SB_WIDTH
OFF_GM = OFF_V + SB_WIDTH
OFF_S5 = OFF_GM + 2 * GM_WIDTH

SUBLANES = 8
LANES = 128

EXP_ZERO_BELOW = -104.0
SB_DEAD_CARRY = -1e30

V7X_VMEM_BYTES = 64 * 1024 * 1024
VMEM_LIMIT = V7X_VMEM_BYTES - 4 * 1024 * 1024


def _cparams(*sem):
    return pltpu.CompilerParams(dimension_semantics=sem, vmem_limit_bytes=VMEM_LIMIT)


def _rms(xf, gain):
    ms = jnp.mean(xf * xf, axis=-1, keepdims=True)
    return xf * lax.rsqrt(ms + RMS_EPS) * gain


def _dot(a, b):
    return jnp.dot(a, b, preferred_element_type=F32)


def _dot_nt(a, b):
    return lax.dot_general(a, b, (((1,), (1,)), ((), ())), preferred_element_type=F32)


def _resident(shape):
    return pl.BlockSpec(shape, lambda *_: (0,) * len(shape), pipeline_mode=pl.Buffered(1))


def _norm_matmul_kernel(x_ref, g_ref, w_ref, o_ref, h_ref):
    @pl.when(pl.program_id(1) == 0)
    def _():
        h_ref[...] = _rms(x_ref[...], g_ref[...]).astype(BF16)

    o_ref[...] = _dot(h_ref[...], w_ref[...]).astype(o_ref.dtype)


def norm_matmul(x, gain, w, *, bm, bn):
    t, d = x.shape
    n = w.shape[1]
    return pl.pallas_call(
        _norm_matmul_kernel,
        grid=(t // bm, n // bn),
        in_specs=[
            pl.BlockSpec((bm, d), lambda i, j: (i, 0)),
            pl.BlockSpec((1, d), lambda i, j: (0, 0)),
            pl.BlockSpec((d, bn), lambda i, j: (0, j)),
        ],
        out_specs=pl.BlockSpec((bm, bn), lambda i, j: (i, j)),
        out_shape=jax.ShapeDtypeStruct((t, n), BF16),
        scratch_shapes=[pltpu.VMEM((bm, d), BF16)],
        compiler_params=_cparams("parallel", "arbitrary"),
        name="norm_matmul",
    )(x, gain.reshape(1, d), w)


def _sb_tile(z, mask, carry, v, uo, bk):
    sp = jnp.log(1.0 + jnp.exp(-jnp.abs(z)))
    ls_neg = jnp.minimum(-z, 0.0) - sp
    ls_pos = z + ls_neg
    if mask is not None:
        ls_neg = jnp.where(mask, ls_neg, 0.0)
    hi = ls_neg.astype(BF16)
    lo = (ls_neg - hi.astype(F32)).astype(BF16)
    cs = _dot(hi, uo) + _dot(lo, uo)
    log_w = ls_pos + cs[:, :bk] + jnp.concatenate([carry] * (bk // LANES), axis=1)
    w = jnp.exp(log_w)
    if mask is not None:
        w = jnp.where(mask, w, 0.0)
    return _dot(w.astype(BF16), v), carry + cs[:, bk:]


def _sb_attn_kernel(q_ref, k_ref, v_ref, uo_ref, o_ref, acc_ref, carry_ref, *, bq, bk, chains, scale):
    i = pl.program_id(2)
    uo = uo_ref[...]
    nd = bq // bk
    acc_ref[...] = jnp.zeros_like(acc_ref)
    carry_ref[...] = jnp.zeros_like(carry_ref)

    for c in range(chains):
        for d in reversed(range(nd)):
            rows = slice(c * bq + d * bk, (c + 1) * bq)
            m = bq - d * bk
            off = pl.multiple_of((i * chains + c) * bq + d * bk, bk)
            z = _dot_nt(q_ref[rows, :], k_ref[pl.ds(off, bk), :]) * scale
            row = lax.broadcasted_iota(jnp.int32, (m, bk), 0)
            col = lax.broadcasted_iota(jnp.int32, (m, bk), 1)
            pv, carry = _sb_tile(z, col < row, carry_ref[rows, :], v_ref[pl.ds(off, bk), :], uo, bk)
            acc_ref[rows, :] += pv
            carry_ref[rows, :] = carry

    first = [(i * chains + c) * nd - 1 for c in range(chains)]

    def cond(state):
        t, lives = state
        live = [jnp.logical_and(first[c] - t >= 0, lives[c] > EXP_ZERO_BELOW) for c in range(chains)]
        return functools.reduce(jnp.logical_or, live)

    def body(state):
        t, _ = state
        pvs, lives = [], []
        for c in range(chains):
            rows = slice(c * bq, (c + 1) * bq)
            kb = first[c] - t
            off = pl.multiple_of(jnp.maximum(kb, 0) * bk, bk)
            carry_in = carry_ref[rows, :] + jnp.where(kb < 0, SB_DEAD_CARRY, 0.0)
            z = _dot_nt(q_ref[rows, :], k_ref[pl.ds(off, bk), :]) * scale
            pv, carry = _sb_tile(z, None, carry_in, v_ref[pl.ds(off, bk), :], uo, bk)
            carry_ref[rows, :] = carry
            pvs.append(pv)
            lives.append(jnp.max(carry))
        acc_ref[...] += jnp.concatenate(pvs, axis=0)
        return t + 1, tuple(lives)

    lax.while_loop(cond, body, body((jnp.int32(0), None)))
    o_ref[...] = acc_ref[...].astype(o_ref.dtype)


def sb_attention(proj, batch, seq, *, bq, bk=LANES, chains=1):
    t = proj.shape[0]
    nq = seq // (bq * chains)
    bq_all = bq * chains
    dh = SB_HEAD_DIM
    r = lax.broadcasted_iota(jnp.int32, (bk, bk + LANES), 0)
    c = lax.broadcasted_iota(jnp.int32, (bk, bk + LANES), 1)
    uo = jnp.where(jnp.logical_or(r > c, c >= bk), 1.0, 0.0).astype(BF16)
    kernel = functools.partial(_sb_attn_kernel, bq=bq, bk=bk, chains=chains, scale=dh ** -0.5)
    return pl.pallas_call(
        kernel,
        grid=(batch, SB_HEADS, nq),
        in_specs=[
            pl.BlockSpec((bq_all, dh), lambda b, h, i: (b * nq + i, OFF_Q // dh + h)),
            pl.BlockSpec((seq, dh), lambda b, h, i: (b, OFF_K // dh + h)),
            pl.BlockSpec((seq, dh), lambda b, h, i: (b, OFF_V // dh + h)),
            pl.BlockSpec((bk, bk + LANES), lambda b, h, i: (0, 0)),
        ],
        out_specs=pl.BlockSpec((bq_all, dh), lambda b, h, i: (b * nq + i, h)),
        out_shape=jax.ShapeDtypeStruct((t, SB_WIDTH), BF16),
        scratch_shapes=[pltpu.VMEM((bq_all, dh), F32), pltpu.VMEM((bq_all, LANES), F32)],
        compiler_params=_cparams("parallel", "parallel", "arbitrary"),
        name="sb_attention",
    )(proj, proj, proj, uo)


def _gm_kernel(uv_ref, gain_ref, w_ref, b_ref, o_ref, *, chunks):
    uv = jax.nn.gelu(uv_ref[...].astype(F32))
    u = uv[:, :GM_WIDTH]
    v = _rms(uv[:, GM_WIDTH:], gain_ref[...]).astype(BF16)
    row = lax.broadcasted_iota(jnp.int32, (GM_CHUNK, GM_CHUNK), 0)
    col = lax.broadcasted_iota(jnp.int32, (GM_CHUNK, GM_CHUNK), 1)
    gd = GM_WIDTH // GM_GROUPS
    for g in range(GM_GROUPS):
        wg = jnp.where(col <= row, w_ref[g], 0.0).astype(BF16)
        bg = b_ref[g]
        for c in range(chunks):
            rs = slice(c * GM_CHUNK, (c + 1) * GM_CHUNK)
            cs = slice(g * gd, (g + 1) * gd)
            mixed = _dot(wg, v[rs, cs]) + bg
            o_ref[rs, cs] = (u[rs, cs] * mixed).astype(o_ref.dtype)


def spatial_gating(proj, v_gain, w_s, b_s, *, bm):
    t = proj.shape[0]
    gd = GM_WIDTH // GM_GROUPS
    b_full = jnp.broadcast_to(b_s[:, :, None], (GM_GROUPS, GM_CHUNK, gd)).astype(F32)
    kernel = functools.partial(_gm_kernel, chunks=bm // GM_CHUNK)
    return pl.pallas_call(
        kernel,
        grid=(t // bm,),
        in_specs=[
            pl.BlockSpec((bm, 2 * GM_WIDTH), lambda i: (i, OFF_GM // (2 * GM_WIDTH))),
            pl.BlockSpec((1, GM_WIDTH), lambda i: (0, 0)),
            pl.BlockSpec((GM_GROUPS, GM_CHUNK, GM_CHUNK), lambda i: (0, 0, 0)),
            pl.BlockSpec((GM_GROUPS, GM_CHUNK, gd), lambda i: (0, 0, 0)),
        ],
        out_specs=pl.BlockSpec((bm, GM_WIDTH), lambda i: (i, 0)),
        out_shape=jax.ShapeDtypeStruct((t, GM_WIDTH), BF16),
        compiler_params=_cparams("parallel"),
        name="spatial_gating",
    )(proj, v_gain.reshape(1, GM_WIDTH), w_s, b_full)


S5_LANE_GROUP = 4
S5_UNROLL = 32


def _s5_lane_chunks():
    n = S5_LANES
    per_half = n // (2 * LANES)
    out = []
    for lc in range(n // LANES):
        hh, k = divmod(lc, per_half)
        out.append((pl.ds(hh * n + k * LANES, LANES), pl.ds(hh * n + n // 2 + k * LANES, LANES)))
    return out


def _cmul_add(pr, pi, sr, si, br, bi):
    return br + (pr * sr - pi * si), bi + (pr * si + pi * sr)


def _s5_kernel(u_ref, perm_ref, permt_ref, wb_ref, pa_ref, pk_ref, wc_ref, d_ref, wg_ref, o_ref,
               x_ref, carry_ref, *, ts):
    n = S5_LANES
    seg = ts // SUBLANES
    chunks = _s5_lane_chunks()

    @pl.when(pl.program_id(1) == 0)
    def _():
        carry_ref[...] = jnp.zeros_like(carry_ref)

    u = _dot(perm_ref[...], u_ref[...]).astype(BF16)
    hw = S5_WIDTH // 2
    for hh in range(2):
        x_ref[:, hh * n:(hh + 1) * n] = _dot(u[:, hh * hw:(hh + 1) * hw], wb_ref[hh])

    row = lax.broadcasted_iota(jnp.int32, (SUBLANES, LANES), 0)
    zero = jnp.zeros((SUBLANES, LANES), F32)
    for g0 in range(0, len(chunks), S5_LANE_GROUP):
        group = chunks[g0:g0 + S5_LANE_GROUP]
        a = [(pa_ref[0, :, re], pa_ref[0, :, im]) for re, im in group]

        def step(k, s, group=group, a=a):
            rows = pl.ds(pl.multiple_of(k * SUBLANES, SUBLANES), SUBLANES)
            out = []
            for (re, im), (ar, ai), (sr, si) in zip(group, a, s):
                nr, ni = _cmul_add(ar, ai, sr, si, x_ref[rows, re], x_ref[rows, im])
                x_ref[rows, re] = nr
                x_ref[rows, im] = ni
                out.append((nr, ni))
            return tuple(out)

        ends = lax.fori_loop(0, seg, step, tuple((zero, zero) for _ in group), unroll=S5_UNROLL)

        enter = []
        for (re, im), (er, ei) in zip(group, ends):
            for tab, shift in ((1, 1), (2, 2), (3, 4)):
                er, ei = _cmul_add(pa_ref[tab, :, re], pa_ref[tab, :, im],
                                   pltpu.roll(er, shift, 0), pltpu.roll(ei, shift, 0), er, ei)
            cr, ci = carry_ref[:, re], carry_ref[:, im]
            er, ei = _cmul_add(pa_ref[4, :, re], pa_ref[4, :, im], cr, ci, er, ei)
            enter.append((jnp.where(row == 0, cr, pltpu.roll(er, 1, 0)),
                          jnp.where(row == 0, ci, pltpu.roll(ei, 1, 0))))
            carry_ref[:, re] = jnp.broadcast_to(er[SUBLANES - 1:, :], (SUBLANES, LANES))
            carry_ref[:, im] = jnp.broadcast_to(ei[SUBLANES - 1:, :], (SUBLANES, LANES))

        def fix(k, c, group=group, enter=enter):
            rows = pl.ds(pl.multiple_of(k * SUBLANES, SUBLANES), SUBLANES)
            for (re, im), (sr, si) in zip(group, enter):
                nr, ni = _cmul_add(pk_ref[k, :, re], pk_ref[k, :, im], sr, si, x_ref[rows, re], x_ref[rows, im])
                x_ref[rows, re] = nr
                x_ref[rows, im] = ni
            return c

        lax.fori_loop(0, seg, fix, 0, unroll=S5_UNROLL)

    y = jnp.concatenate([_dot(x_ref[:, hh * n:(hh + 1) * n].astype(BF16), wc_ref[hh]) for hh in range(2)], axis=-1)
    y = jax.nn.gelu(y + d_ref[...] * u.astype(F32)).astype(BF16)
    ag = _dot(y, wg_ref[...])
    out = (ag[:, :S5_WIDTH] * jax.nn.sigmoid(ag[:, S5_WIDTH:])).astype(BF16)
    o_ref[...] = _dot(permt_ref[...], out).astype(o_ref.dtype)


def _s5_tables(lam_re, lam_im, log_dt, b_re, b_im, c_re, c_im, seg):
    g, n, c = S5_GROUPS, S5_STATE, S5_GROUP_DIM
    lr, li = lam_re.astype(F32), lam_im.astype(F32)
    dt = jnp.exp(log_dt.astype(F32))[:, None]
    mag = jnp.exp(lr * dt)
    a_re, a_im = mag * jnp.cos(li * dt), mag * jnp.sin(li * dt)
    den = lr * lr + li * li
    x_ = a_re - 1.0
    f_re = (x_ * lr + a_im * li) / den
    f_im = (a_im * lr - x_ * li) / den
    br, bi = b_re.astype(F32), b_im.astype(F32)
    bb_re = f_re[..., None] * br - f_im[..., None] * bi
    bb_im = f_re[..., None] * bi + f_im[..., None] * br
    gh = g // 2
    eye = jnp.eye(gh, dtype=F32)

    def in_map(bb):
        return jnp.einsum('hgnc,gk->hgckn', bb.reshape(2, gh, n, c), eye).reshape(2, gh * c, gh * n)

    def out_map(cc):
        return jnp.einsum('hgcn,gk->hgnkc', cc.astype(F32).reshape(2, gh, c, n), eye).reshape(2, gh * n, gh * c)

    wb = jnp.concatenate([in_map(bb_re), in_map(bb_im)], axis=2).astype(BF16)
    wc = jnp.concatenate([out_map(c_re), -out_map(c_im)], axis=1).astype(BF16)

    def lanes(pr, pi):
        hl = gh * n
        return jnp.concatenate([pr[..., :hl], pi[..., :hl], pr[..., hl:], pi[..., hl:]], axis=-1)

    exps = list(range(1, seg + 1)) + [seg * m for m in (1, 2, 4)] + [seg * (j + 1) for j in range(SUBLANES)]
    sq = [(a_re.reshape(1, -1), a_im.reshape(1, -1))]
    for _ in range(max(exps).bit_length() - 1):
        sr, si = sq[-1]
        sq.append((sr * sr - si * si, 2.0 * sr * si))
    e = jnp.asarray(exps, jnp.int32)[:, None]
    pr, pi = jnp.ones((len(exps), g * n), F32), jnp.zeros((len(exps), g * n), F32)
    for bit, (sr, si) in enumerate(sq):
        on = ((e >> bit) & 1) == 1
        pr, pi = jnp.where(on, pr * sr - pi * si, pr), jnp.where(on, pr * si + pi * sr, pi)
    pw = lanes(pr, pi)
    rows = jnp.arange(SUBLANES)[:, None]
    pa = [jnp.broadcast_to(pw[0][None], (SUBLANES, pw.shape[-1]))]
    for m, shift in enumerate((1, 2, 4)):
        pa.append(jnp.where(rows >= shift, pw[seg + m][None], 0.0))
    pa.append(pw[seg + 3:])
    pk = jnp.broadcast_to(pw[:seg, None, :], (seg, SUBLANES, pw.shape[-1]))
    return wb, wc, jnp.stack(pa).astype(F32), pk.astype(F32)


def s5_mixer(proj, batch, seq, wb, wc, pa, pk, d_skip, w_glu, *, ts):
    t = proj.shape[0]
    nt = seq // ts
    n2 = 2 * S5_LANES
    seg = ts // SUBLANES
    new = jnp.arange(ts)
    perm = (new[:, None] % SUBLANES * seg + new[:, None] // SUBLANES == new[None, :]).astype(BF16)
    kernel = functools.partial(_s5_kernel, ts=ts)
    return pl.pallas_call(
        kernel,
        grid=(batch, nt),
        in_specs=[
            pl.BlockSpec((ts, S5_WIDTH), lambda b, i: (b * nt + i, OFF_S5 // S5_WIDTH)),
            _resident((ts, ts)),
            _resident((ts, ts)),
            _resident((2, S5_WIDTH // 2, S5_LANES)),
            _resident((5, SUBLANES, n2)),
            _resident((seg, SUBLANES, n2)),
            _resident((2, S5_LANES, S5_WIDTH // 2)),
            _resident((1, S5_WIDTH)),
            _resident((S5_WIDTH, 2 * S5_WIDTH)),
        ],
        out_specs=pl.BlockSpec((ts, S5_WIDTH), lambda b, i: (b * nt + i, 0)),
        out_shape=jax.ShapeDtypeStruct((t, S5_WIDTH), BF16),
        scratch_shapes=[pltpu.VMEM((ts, n2), F32), pltpu.VMEM((SUBLANES, n2), F32)],
        compiler_params=_cparams("parallel", "arbitrary"),
        name="s5_mixer",
    )(proj, perm, perm.T, wb, pa, pk, wc, d_skip.reshape(1, S5_WIDTH).astype(F32), w_glu)


def _merge_kernel(ya_ref, yb_ref, yc_ref, g0_ref, g1_ref, g2_ref, x_ref, wa_ref, wb_ref, wc_ref,
                  wo_ref, gain_ref, o_ref, m_ref, *, cw):
    ya, yb, yc = ya_ref[...], yb_ref[...], yc_ref[...]
    for k in range(m_ref.shape[1] // cw):
        cs = slice(k * cw, (k + 1) * cw)
        m = (jax.nn.sigmoid(g0_ref[:, cs].astype(F32)) * _dot(ya, wa_ref[:, cs])
             + jax.nn.sigmoid(g1_ref[:, cs].astype(F32)) * _dot(yb, wb_ref[:, cs])
             + jax.nn.sigmoid(g2_ref[:, cs].astype(F32)) * _dot(yc, wc_ref[:, cs]))
        m_ref[:, cs] = m.astype(BF16)
    y = _dot(m_ref[...], wo_ref[...])
    o_ref[...] = x_ref[...] + _rms(y, gain_ref[...])


def merge_out(y_a, y_b, y_c, proj, x, w_a, w_b, w_c, w_o, gain, *, bm, cw=512):
    t, d = x.shape
    g0 = OFF_GATE // d
    kernel = functools.partial(_merge_kernel, cw=cw)
    return pl.pallas_call(
        kernel,
        grid=(t // bm,),
        in_specs=[
            pl.BlockSpec((bm, SB_WIDTH), lambda i: (i, 0)),
            pl.BlockSpec((bm, GM_WIDTH), lambda i: (i, 0)),
            pl.BlockSpec((bm, S5_WIDTH), lambda i: (i, 0)),
            pl.BlockSpec((bm, d), lambda i: (i, g0)),
            pl.BlockSpec((bm, d), lambda i: (i, g0 + 1)),
            pl.BlockSpec((bm, d), lambda i: (i, g0 + 2)),
            pl.BlockSpec((bm, d), lambda i: (i, 0)),
            _resident((SB_WIDTH, d)),
            _resident((GM_WIDTH, d)),
            _resident((S5_WIDTH, d)),
            _resident((d, d)),
            _resident((1, d)),
        ],
        out_specs=pl.BlockSpec((bm, d), lambda i: (i, 0)),
        out_shape=jax.ShapeDtypeStruct((t, d), F32),
        scratch_shapes=[pltpu.VMEM((bm, d), BF16)],
        compiler_params=_cparams("parallel"),
        name="merge_out",
    )(y_a, y_b, y_c, proj, proj, proj, x, w_a, w_b, w_c, w_o, gain.reshape(1, d))


def _xattn_kernel(x_ref, g_ref, wq_ref, kv_ref, wo_ref, pg_ref, o_ref, *, scale):
    x = x_ref[...]
    h = _rms(x, g_ref[...]).astype(BF16)
    q = (_dot(h, wq_ref[...]) * scale).astype(BF16)
    outs = []
    for hd in range(X_HEADS):
        cs = slice(hd * X_HEAD_DIM, (hd + 1) * X_HEAD_DIM)
        vs = slice(X_WIDTH + hd * X_HEAD_DIM, X_WIDTH + (hd + 1) * X_HEAD_DIM)
        z = _dot_nt(q[:, cs], kv_ref[:, cs])
        e = jnp.exp(z - jnp.max(z, axis=-1, keepdims=True))
        p = e / jnp.sum(e, axis=-1, keepdims=True)
        outs.append(_dot(p.astype(BF16), kv_ref[:, vs]))
    o = jnp.concatenate(outs, axis=-1).astype(BF16)
    o_ref[...] = x + _rms(_dot(o, wo_ref[...]), pg_ref[...])


def cross_attention(x, kv, seq, gain, w_q, w_o, post_gain, *, bm):
    t, d = x.shape
    mem_len = kv.shape[0] // (t // seq)
    per_batch = seq // bm
    kernel = functools.partial(_xattn_kernel, scale=X_HEAD_DIM ** -0.5)
    return pl.pallas_call(
        kernel,
        grid=(t // bm,),
        in_specs=[
            pl.BlockSpec((bm, d), lambda i: (i, 0)),
            pl.BlockSpec((1, d), lambda i: (0, 0)),
            pl.BlockSpec((d, X_WIDTH), lambda i: (0, 0)),
            pl.BlockSpec((mem_len, 2 * X_WIDTH), lambda i: (i // per_batch, 0)),
            pl.BlockSpec((X_WIDTH, d), lambda i: (0, 0)),
            pl.BlockSpec((1, d), lambda i: (0, 0)),
        ],
        out_specs=pl.BlockSpec((bm, d), lambda i: (i, 0)),
        out_shape=jax.ShapeDtypeStruct((t, d), F32),
        compiler_params=_cparams("parallel"),
        name="cross_attention",
    )(x, gain.reshape(1, d), w_q, kv, w_o, post_gain.reshape(1, d))


def _ffn_kernel(x_ref, g_ref, wg_ref, wu_ref, wo_ref, pg_ref, o_ref, h_ref, *, nf):
    f = pl.program_id(1)

    @pl.when(f == 0)
    def _():
        h_ref[...] = _rms(x_ref[...], g_ref[...]).astype(BF16)
        o_ref[...] = jnp.zeros_like(o_ref)

    nb = wg_ref.shape[1] // LANES
    wi = jnp.concatenate(
        [w[:, k * LANES:(k + 1) * LANES] for k in range(nb) for w in (wg_ref, wu_ref)], axis=1)
    gu = _dot(h_ref[...], wi)
    a = jnp.concatenate(
        [jax.nn.silu(gu[:, 2 * k * LANES:(2 * k + 1) * LANES]) * gu[:, (2 * k + 1) * LANES:(2 * k + 2) * LANES]
         for k in range(nb)], axis=1).astype(BF16)
    o_ref[...] += _dot(a, wo_ref[...])

    @pl.when(f == nf - 1)
    def _():
        o_ref[...] = x_ref[...] + _rms(o_ref[...], pg_ref[...])


def ffn(x, gain, w_in, w_out, post_gain, *, bm, bf):
    t, d = x.shape
    hidden = w_out.shape[0]
    nf = hidden // bf
    kernel = functools.partial(_ffn_kernel, nf=nf)
    return pl.pallas_call(
        kernel,
        grid=(t // bm, nf),
        in_specs=[
            pl.BlockSpec((bm, d), lambda i, f: (i, 0)),
            pl.BlockSpec((1, d), lambda i, f: (0, 0)),
            pl.BlockSpec((d, bf), lambda i, f: (0, f)),
            pl.BlockSpec((d, bf), lambda i, f: (0, nf + f)),
            pl.BlockSpec((bf, d), lambda i, f: (f, 0)),
            pl.BlockSpec((1, d), lambda i, f: (0, 0)),
        ],
        out_specs=pl.BlockSpec((bm, d), lambda i, f: (i, 0)),
        out_shape=jax.ShapeDtypeStruct((t, d), F32),
        scratch_shapes=[pltpu.VMEM((bm, d), BF16)],
        compiler_params=_cparams("parallel", "arbitrary"),
        name="ffn",
    )(x, gain.reshape(1, d), w_in, w_in, w_out, post_gain.reshape(1, d))


def kernel(x, mem, mix_pre_gain, mix_post_gain, w_in, gm_v_gain, gm_w_s, gm_b_s, s5_lam_re, s5_lam_im, s5_log_dt, s5_b_re, s5_b_im, s5_c_re, s5_c_im, s5_d, s5_w_glu, w_br_a, w_br_b, w_br_c, w_out, xattn_pre_gain, xattn_post_gain, mem_gain, w_xq, w_xkv, w_xo, ffn_pre_gain, ffn_post_gain, w_ffn_in, w_ffn_out):
    batch, seq, d = x.shape
    depth = w_in.shape[0]
    t = batch * seq
    bm = math.gcd(seq, 512)
    xs = x.reshape(t, d)
    mem2 = mem.reshape(batch * mem.shape[1], d)
    n_in = w_in.shape[2]
    bn = n_in // 3 if n_in % (3 * LANES) == 0 else n_in
    assert d == D_MODEL and n_in == W_IN_GATE_COL + N_BRANCH * D_MODEL
    for l in range(depth):
        w16 = lambda w: w[l].astype(BF16)
        w_in_l = jnp.concatenate([w_in[l][:, W_IN_GATE_COL:], w_in[l][:, :W_IN_GATE_COL]], axis=1).astype(BF16)
        proj = norm_matmul(xs, mix_pre_gain[l], w_in_l, bm=bm, bn=bn)
        y_a = sb_attention(proj, batch, seq, bq=math.gcd(seq, 256), bk=256, chains=4)
        y_b = spatial_gating(proj, gm_v_gain[l], gm_w_s[l], gm_b_s[l], bm=bm)
        ts = math.gcd(seq, 256)
        wb, wc, pa, pk = _s5_tables(s5_lam_re[l], s5_lam_im[l], s5_log_dt[l], s5_b_re[l], s5_b_im[l],
                                    s5_c_re[l], s5_c_im[l], ts // SUBLANES)
        y_c = s5_mixer(proj, batch, seq, wb, wc, pa, pk, s5_d[l], w16(s5_w_glu), ts=ts)
        xs = merge_out(y_a, y_b, y_c, proj, xs, w16(w_br_a), w16(w_br_b), w16(w_br_c), w16(w_out),
                       mix_post_gain[l], bm=bm)
        kv = norm_matmul(mem2, mem_gain[l], w16(w_xkv), bm=math.gcd(mem2.shape[0], 256), bn=2 * X_WIDTH)
        xs = cross_attention(xs, kv, seq, xattn_pre_gain[l], w16(w_xq), w16(w_xo), xattn_post_gain[l], bm=bm)
        xs = ffn(xs, ffn_pre_gain[l], w16(w_ffn_in), w16(w_ffn_out), ffn_post_gain[l], bm=bm, bf=512)
    return xs.reshape(batch, seq, d)
```

```python
import functools
import math

import jax
import jax.numpy as jnp
from jax import lax
from jax.experimental import pallas as pl
from jax.experimental.pallas import tpu as pltpu

F32 = jnp.float32
BF16 = jnp.bfloat16

RMS_EPS = 1e-6

SB_HEADS = 4
SB_HEAD_DIM = 256
SB_WIDTH = SB_HEADS * SB_HEAD_DIM
GM_GROUPS = 4
GM_CHUNK = 128
GM_WIDTH = 512
S5_WIDTH = 512
S5_GROUP_DIM = 16
S5_GROUPS = S5_WIDTH // S5_GROUP_DIM
S5_STATE = 64
S5_LANES = S5_GROUPS * S5_STATE
X_HEADS = 4
X_HEAD_DIM = 128
X_WIDTH = X_HEADS * X_HEAD_DIM

D_MODEL = 2048
N_BRANCH = 3

W_IN_GATE_COL = 3 * SB_WIDTH + 2 * GM_WIDTH + S5_WIDTH
OFF_GATE = 0
OFF_Q = N_BRANCH * D_MODEL
OFF_K = OFF_Q + SB_WIDTH
OFF_V = OFF_K + SB_WIDTH
OFF_GM = OFF_V + SB_WIDTH
OFF_S5 = OFF_GM + 2 * GM_WIDTH

SUBLANES = 8
LANES = 128

EXP_ZERO_BELOW = -104.0
SB_DEAD_CARRY = -1e30

V7X_VMEM_BYTES = 64 * 1024 * 1024
VMEM_LIMIT = V7X_VMEM_BYTES - 4 * 1024 * 1024


def _cparams(*sem):
    return pltpu.CompilerParams(dimension_semantics=sem, vmem_limit_bytes=VMEM_LIMIT)


def _rms(xf, gain):
    ms = jnp.mean(xf * xf, axis=-1, keepdims=True)
    return xf * lax.rsqrt(ms + RMS_EPS) * gain


def _dot(a, b):
    return jnp.dot(a, b, preferred_element_type=F32)


def _dot_nt(a, b):
    return lax.dot_general(a, b, (((1,), (1,)), ((), ())), preferred_element_type=F32)


def _resident(shape, layer=None):
    if layer is None:
        return pl.BlockSpec(shape, lambda *_: (0,) * len(shape), pipeline_mode=pl.Buffered(1))
    return pl.BlockSpec((None,) + tuple(shape), lambda *_: (layer,) + (0,) * len(shape),
                        pipeline_mode=pl.Buffered(1))


def _col_tiles(w, bn):
    *lead, d, n = w.shape
    k = len(lead)
    return w.reshape(*lead, d, n // bn, bn).transpose(*range(k), k + 1, k, k + 2)


def _norm_matmul_kernel(x_ref, g_ref, w_ref, o_ref, h_ref):
    @pl.when(pl.program_id(1) == 0)
    def _():
        h_ref[...] = _rms(x_ref[...], g_ref[...]).astype(BF16)

    o_ref[...] = _dot(h_ref[...], w_ref[...]).astype(o_ref.dtype)


def norm_matmul(x, gain, w_tiles, layer, *, bm):
    t, d = x.shape
    _, nj, _, bn = w_tiles.shape
    n = nj * bn
    return pl.pallas_call(
        _norm_matmul_kernel,
        grid=(t // bm, nj),
        in_specs=[
            pl.BlockSpec((bm, d), lambda i, j: (i, 0)),
            pl.BlockSpec((1, d), lambda i, j: (0, 0)),
            pl.BlockSpec((None, None, d, bn), lambda i, j: (layer, j, 0, 0)),
        ],
        out_specs=pl.BlockSpec((bm, bn), lambda i, j: (i, j)),
        out_shape=jax.ShapeDtypeStruct((t, n), BF16),
        scratch_shapes=[pltpu.VMEM((bm, d), BF16)],
        compiler_params=_cparams("parallel", "arbitrary"),
        name="norm_matmul",
    )(x, gain.reshape(1, d), w_tiles)


def _sb_tile(z, mask, carry, v, uo, bk):
    sp = jnp.log(1.0 + jnp.exp(-jnp.abs(z)))
    ls_neg = jnp.minimum(-z, 0.0) - sp
    ls_pos = z + ls_neg
    if mask is not None:
        ls_neg = jnp.where(mask, ls_neg, 0.0)
    hi = ls_neg.astype(BF16)
    lo = (ls_neg - hi.astype(F32)).astype(BF16)
    cs = _dot(hi, uo) + _dot(lo, uo)
    log_w = ls_pos + cs[:, :bk] + jnp.concatenate([carry] * (bk // LANES), axis=1)
    w = jnp.exp(log_w)
    if mask is not None:
        w = jnp.where(mask, w, 0.0)
    return _dot(w.astype(BF16), v), carry + cs[:, bk:]


def _sb_attn_kernel(q_ref, k_ref, v_ref, uo_ref, o_ref, acc_ref, carry_ref, *, bq, bk, chains, scale):
    i = pl.program_id(2)
    uo = uo_ref[...]
    nd = bq // bk
    acc_ref[...] = jnp.zeros_like(acc_ref)
    carry_ref[...] = jnp.zeros_like(carry_ref)

    for c in range(chains):
        for d in reversed(range(nd)):
            rows = slice(c * bq + d * bk, (c + 1) * bq)
            m = bq - d * bk
            off = pl.multiple_of((i * chains + c) * bq + d * bk, bk)
            z = _dot_nt(q_ref[rows, :], k_ref[pl.ds(off, bk), :]) * scale
            row = lax.broadcasted_iota(jnp.int32, (m, bk), 0)
            col = lax.broadcasted_iota(jnp.int32, (m, bk), 1)
            pv, carry = _sb_tile(z, col < row, carry_ref[rows, :], v_ref[pl.ds(off, bk), :], uo, bk)
            acc_ref[rows, :] += pv
            carry_ref[rows, :] = carry

    first = [(i * chains + c) * nd - 1 for c in range(chains)]

    def cond(state):
        t, lives = state
        live = [jnp.logical_and(first[c] - t >= 0, lives[c] > EXP_ZERO_BELOW) for c in range(chains)]
        return functools.reduce(jnp.logical_or, live)

    def body(state):
        t, _ = state
        pvs, lives = [], []
        for c in range(chains):
            rows = slice(c * bq, (c + 1) * bq)
            kb = first[c] - t
            off = pl.multiple_of(jnp.maximum(kb, 0) * bk, bk)
            carry_in = carry_ref[rows, :] + jnp.where(kb < 0, SB_DEAD_CARRY, 0.0)
            z = _dot_nt(q_ref[rows, :], k_ref[pl.ds(off, bk), :]) * scale
            pv, carry = _sb_tile(z, None, carry_in, v_ref[pl.ds(off, bk), :], uo, bk)
            carry_ref[rows, :] = carry
            pvs.append(pv)
            lives.append(jnp.max(carry))
        acc_ref[...] += jnp.concatenate(pvs, axis=0)
        return t + 1, tuple(lives)

    lax.while_loop(cond, body, body((jnp.int32(0), None)))
    o_ref[...] = acc_ref[...].astype(o_ref.dtype)


def sb_attention(proj, batch, seq, *, bq, bk=LANES, chains=1):
    t = proj.shape[0]
    nq = seq // (bq * chains)
    bq_all = bq * chains
    dh = SB_HEAD_DIM
    r = lax.broadcasted_iota(jnp.int32, (bk, bk + LANES), 0)
    c = lax.broadcasted_iota(jnp.int32, (bk, bk + LANES), 1)
    uo = jnp.where(jnp.logical_or(r > c, c >= bk), 1.0, 0.0).astype(BF16)
    kernel = functools.partial(_sb_attn_kernel, bq=bq, bk=bk, chains=chains, scale=dh ** -0.5)
    return pl.pallas_call(
        kernel,
        grid=(batch, SB_HEADS, nq),
        in_specs=[
            pl.BlockSpec((bq_all, dh), lambda b, h, i: (b * nq + i, OFF_Q // dh + h)),
            pl.BlockSpec((seq, dh), lambda b, h, i: (b, OFF_K // dh + h)),
            pl.BlockSpec((seq, dh), lambda b, h, i: (b, OFF_V // dh + h)),
            pl.BlockSpec((bk, bk + LANES), lambda b, h, i: (0, 0)),
        ],
        out_specs=pl.BlockSpec((bq_all, dh), lambda b, h, i: (b * nq + i, h)),
        out_shape=jax.ShapeDtypeStruct((t, SB_WIDTH), BF16),
        scratch_shapes=[pltpu.VMEM((bq_all, dh), F32), pltpu.VMEM((bq_all, LANES), F32)],
        compiler_params=_cparams("parallel", "parallel", "arbitrary"),
        name="sb_attention",
    )(proj, proj, proj, uo)


def _gm_kernel(uv_ref, gain_ref, w_ref, b_ref, o_ref, *, chunks):
    uv = jax.nn.gelu(uv_ref[...].astype(F32))
    u = uv[:, :GM_WIDTH]
    v = _rms(uv[:, GM_WIDTH:], gain_ref[...]).astype(BF16)
    row = lax.broadcasted_iota(jnp.int32, (GM_CHUNK, GM_CHUNK), 0)
    col = lax.broadcasted_iota(jnp.int32, (GM_CHUNK, GM_CHUNK), 1)
    gd = GM_WIDTH // GM_GROUPS
    for g in range(GM_GROUPS):
        wg = jnp.where(col <= row, w_ref[g], 0.0).astype(BF16)
        bg = b_ref[g]
        for c in range(chunks):
            rs = slice(c * GM_CHUNK, (c + 1) * GM_CHUNK)
            cs = slice(g * gd, (g + 1) * gd)
            mixed = _dot(wg, v[rs, cs]) + bg
            o_ref[rs, cs] = (u[rs, cs] * mixed).astype(o_ref.dtype)


def spatial_gating(proj, v_gain, w_s, b_s, *, bm):
    t = proj.shape[0]
    gd = GM_WIDTH // GM_GROUPS
    b_full = jnp.broadcast_to(b_s[:, :, None], (GM_GROUPS, GM_CHUNK, gd)).astype(F32)
    kernel = functools.partial(_gm_kernel, chunks=bm // GM_CHUNK)
    return pl.pallas_call(
        kernel,
        grid=(t // bm,),
        in_specs=[
            pl.BlockSpec((bm, 2 * GM_WIDTH), lambda i: (i, OFF_GM // (2 * GM_WIDTH))),
            pl.BlockSpec((1, GM_WIDTH), lambda i: (0, 0)),
            pl.BlockSpec((GM_GROUPS, GM_CHUNK, GM_CHUNK), lambda i: (0, 0, 0)),
            pl.BlockSpec((GM_GROUPS, GM_CHUNK, gd), lambda i: (0, 0, 0)),
        ],
        out_specs=pl.BlockSpec((bm, GM_WIDTH), lambda i: (i, 0)),
        out_shape=jax.ShapeDtypeStruct((t, GM_WIDTH), BF16),
        compiler_params=_cparams("parallel"),
        name="spatial_gating",
    )(proj, v_gain.reshape(1, GM_WIDTH), w_s, b_full)


S5_LANE_GROUP = 4
S5_UNROLL = 32


def _s5_lane_chunks():
    n = S5_LANES
    per_half = n // (2 * LANES)
    out = []
    for lc in range(n // LANES):
        hh, k = divmod(lc, per_half)
        out.append((pl.ds(hh * n + k * LANES, LANES), pl.ds(hh * n + n // 2 + k * LANES, LANES)))
    return out


def _cmul_add(pr, pi, sr, si, br, bi):
    return br + (pr * sr - pi * si), bi + (pr * si + pi * sr)


def _s5_kernel(u_ref, perm_ref, permt_ref, wb_ref, pa_ref, pk_ref, wc_ref, d_ref, wg_ref, o_ref,
               x_ref, carry_ref, *, ts):
    n = S5_LANES
    seg = ts // SUBLANES
    chunks = _s5_lane_chunks()

    @pl.when(pl.program_id(1) == 0)
    def _():
        carry_ref[...] = jnp.zeros_like(carry_ref)

    u = _dot(perm_ref[...], u_ref[...]).astype(BF16)
    hw = S5_WIDTH // 2
    for hh in range(2):
        x_ref[:, hh * n:(hh + 1) * n] = _dot(u[:, hh * hw:(hh + 1) * hw], wb_ref[hh])

    row = lax.broadcasted_iota(jnp.int32, (SUBLANES, LANES), 0)
    zero = jnp.zeros((SUBLANES, LANES), F32)
    for g0 in range(0, len(chunks), S5_LANE_GROUP):
        group = chunks[g0:g0 + S5_LANE_GROUP]
        a = [(pa_ref[0, :, re], pa_ref[0, :, im]) for re, im in group]

        def step(k, s, group=group, a=a):
            rows = pl.ds(pl.multiple_of(k * SUBLANES, SUBLANES), SUBLANES)
            out = []
            for (re, im), (ar, ai), (sr, si) in zip(group, a, s):
                nr, ni = _cmul_add(ar, ai, sr, si, x_ref[rows, re], x_ref[rows, im])
                x_ref[rows, re] = nr
                x_ref[rows, im] = ni
                out.append((nr, ni))
            return tuple(out)

        ends = lax.fori_loop(0, seg, step, tuple((zero, zero) for _ in group), unroll=S5_UNROLL)

        enter = []
        for (re, im), (er, ei) in zip(group, ends):
            for tab, shift in ((1, 1), (2, 2), (3, 4)):
                er, ei = _cmul_add(pa_ref[tab, :, re], pa_ref[tab, :, im],
                                   pltpu.roll(er, shift, 0), pltpu.roll(ei, shift, 0), er, ei)
            cr, ci = carry_ref[:, re], carry_ref[:, im]
            er, ei = _cmul_add(pa_ref[4, :, re], pa_ref[4, :, im], cr, ci, er, ei)
            enter.append((jnp.where(row == 0, cr, pltpu.roll(er, 1, 0)),
                          jnp.where(row == 0, ci, pltpu.roll(ei, 1, 0))))
            carry_ref[:, re] = jnp.broadcast_to(er[SUBLANES - 1:, :], (SUBLANES, LANES))
            carry_ref[:, im] = jnp.broadcast_to(ei[SUBLANES - 1:, :], (SUBLANES, LANES))

        def fix(k, c, group=group, enter=enter):
            rows = pl.ds(pl.multiple_of(k * SUBLANES, SUBLANES), SUBLANES)
            for (re, im), (sr, si) in zip(group, enter):
                nr, ni = _cmul_add(pk_ref[k, :, re], pk_ref[k, :, im], sr, si, x_ref[rows, re], x_ref[rows, im])
                x_ref[rows, re] = nr
                x_ref[rows, im] = ni
            return c

        lax.fori_loop(0, seg, fix, 0, unroll=S5_UNROLL)

    y = jnp.concatenate([_dot(x_ref[:, hh * n:(hh + 1) * n].astype(BF16), wc_ref[hh]) for hh in range(2)], axis=-1)
    y = jax.nn.gelu(y + d_ref[...] * u.astype(F32)).astype(BF16)
    ag = _dot(y, wg_ref[...])
    out = (ag[:, :S5_WIDTH] * jax.nn.sigmoid(ag[:, S5_WIDTH:])).astype(BF16)
    o_ref[...] = _dot(permt_ref[...], out).astype(o_ref.dtype)


def _s5_tables(lam_re, lam_im, log_dt, b_re, b_im, c_re, c_im, seg):
    g, n, c = S5_GROUPS, S5_STATE, S5_GROUP_DIM
    lr, li = lam_re.astype(F32), lam_im.astype(F32)
    dt = jnp.exp(log_dt.astype(F32))[:, None]
    mag = jnp.exp(lr * dt)
    a_re, a_im = mag * jnp.cos(li * dt), mag * jnp.sin(li * dt)
    den = lr * lr + li * li
    x_ = a_re - 1.0
    f_re = (x_ * lr + a_im * li) / den
    f_im = (a_im * lr - x_ * li) / den
    br, bi = b_re.astype(F32), b_im.astype(F32)
    bb_re = f_re[..., None] * br - f_im[..., None] * bi
    bb_im = f_re[..., None] * bi + f_im[..., None] * br
    gh = g // 2
    eye = jnp.eye(gh, dtype=F32)

    def in_map(bb):
        return jnp.einsum('hgnc,gk->hgckn', bb.reshape(2, gh, n, c), eye).reshape(2, gh * c, gh * n)

    def out_map(cc):
        return jnp.einsum('hgcn,gk->hgnkc', cc.astype(F32).reshape(2, gh, c, n), eye).reshape(2, gh * n, gh * c)

    wb = jnp.concatenate([in_map(bb_re), in_map(bb_im)], axis=2).astype(BF16)
    wc = jnp.concatenate([out_map(c_re), -out_map(c_im)], axis=1).astype(BF16)

    def lanes(pr, pi):
        hl = gh * n
        return jnp.concatenate([pr[..., :hl], pi[..., :hl], pr[..., hl:], pi[..., hl:]], axis=-1)

    exps = list(range(1, seg + 1)) + [seg * m for m in (1, 2, 4)] + [seg * (j + 1) for j in range(SUBLANES)]
    sq = [(a_re.reshape(1, -1), a_im.reshape(1, -1))]
    for _ in range(max(exps).bit_length() - 1):
        sr, si = sq[-1]
        sq.append((sr * sr - si * si, 2.0 * sr * si))
    e = jnp.asarray(exps, jnp.int32)[:, None]
    pr, pi = jnp.ones((len(exps), g * n), F32), jnp.zeros((len(exps), g * n), F32)
    for bit, (sr, si) in enumerate(sq):
        on = ((e >> bit) & 1) == 1
        pr, pi = jnp.where(on, pr * sr - pi * si, pr), jnp.where(on, pr * si + pi * sr, pi)
    pw = lanes(pr, pi)
    rows = jnp.arange(SUBLANES)[:, None]
    pa = [jnp.broadcast_to(pw[0][None], (SUBLANES, pw.shape[-1]))]
    for m, shift in enumerate((1, 2, 4)):
        pa.append(jnp.where(rows >= shift, pw[seg + m][None], 0.0))
    pa.append(pw[seg + 3:])
    pk = jnp.broadcast_to(pw[:seg, None, :], (seg, SUBLANES, pw.shape[-1]))
    return wb, wc, jnp.stack(pa).astype(F32), pk.astype(F32)


def s5_mixer(proj, batch, seq, wb, wc, pa, pk, d_skip, w_glu, layer, *, ts):
    t = proj.shape[0]
    nt = seq // ts
    n2 = 2 * S5_LANES
    seg = ts // SUBLANES
    new = jnp.arange(ts)
    perm = (new[:, None] % SUBLANES * seg + new[:, None] // SUBLANES == new[None, :]).astype(BF16)
    kernel = functools.partial(_s5_kernel, ts=ts)
    return pl.pallas_call(
        kernel,
        grid=(batch, nt),
        in_specs=[
            pl.BlockSpec((ts, S5_WIDTH), lambda b, i: (b * nt + i, OFF_S5 // S5_WIDTH)),
            _resident((ts, ts)),
            _resident((ts, ts)),
            _resident((2, S5_WIDTH // 2, S5_LANES), layer),
            _resident((5, SUBLANES, n2), layer),
            _resident((seg, SUBLANES, n2), layer),
            _resident((2, S5_LANES, S5_WIDTH // 2), layer),
            _resident((1, S5_WIDTH), layer),
            _resident((S5_WIDTH, 2 * S5_WIDTH), layer),
        ],
        out_specs=pl.BlockSpec((ts, S5_WIDTH), lambda b, i: (b * nt + i, 0)),
        out_shape=jax.ShapeDtypeStruct((t, S5_WIDTH), BF16),
        scratch_shapes=[pltpu.VMEM((ts, n2), F32), pltpu.VMEM((SUBLANES, n2), F32)],
        compiler_params=_cparams("parallel", "arbitrary"),
        name="s5_mixer",
    )(proj, perm, perm.T, wb, pa, pk, wc, d_skip.reshape(-1, 1, S5_WIDTH).astype(F32), w_glu)


def _merge_kernel(ya_ref, yb_ref, yc_ref, g0_ref, g1_ref, g2_ref, x_ref, wa_ref, wb_ref, wc_ref,
                  wo_ref, gain_ref, o_ref, m_ref, *, cw):
    ya, yb, yc = ya_ref[...], yb_ref[...], yc_ref[...]
    for k in range(m_ref.shape[1] // cw):
        cs = slice(k * cw, (k + 1) * cw)
        m = (jax.nn.sigmoid(g0_ref[:, cs].astype(F32)) * _dot(ya, wa_ref[:, cs])
             + jax.nn.sigmoid(g1_ref[:, cs].astype(F32)) * _dot(yb, wb_ref[:, cs])
             + jax.nn.sigmoid(g2_ref[:, cs].astype(F32)) * _dot(yc, wc_ref[:, cs]))
        m_ref[:, cs] = m.astype(BF16)
    y = _dot(m_ref[...], wo_ref[...])
    o_ref[...] = x_ref[...] + _rms(y, gain_ref[...])


def merge_out(y_a, y_b, y_c, proj, x, w_a, w_b, w_c, w_o, gain, layer, *, bm, cw=512):
    t, d = x.shape
    g0 = OFF_GATE // d
    kernel = functools.partial(_merge_kernel, cw=cw)
    return pl.pallas_call(
        kernel,
        grid=(t // bm,),
        in_specs=[
            pl.BlockSpec((bm, SB_WIDTH), lambda i: (i, 0)),
            pl.BlockSpec((bm, GM_WIDTH), lambda i: (i, 0)),
            pl.BlockSpec((bm, S5_WIDTH), lambda i: (i, 0)),
            pl.BlockSpec((bm, d), lambda i: (i, g0)),
            pl.BlockSpec((bm, d), lambda i: (i, g0 + 1)),
            pl.BlockSpec((bm, d), lambda i: (i, g0 + 2)),
            pl.BlockSpec((bm, d), lambda i: (i, 0)),
            _resident((SB_WIDTH, d), layer),
            _resident((GM_WIDTH, d), layer),
            _resident((S5_WIDTH, d), layer),
            _resident((d, d), layer),
            _resident((1, d)),
        ],
        out_specs=pl.BlockSpec((bm, d), lambda i: (i, 0)),
        out_shape=jax.ShapeDtypeStruct((t, d), F32),
        scratch_shapes=[pltpu.VMEM((bm, d), BF16)],
        compiler_params=_cparams("parallel"),
        name="merge_out",
    )(y_a, y_b, y_c, proj, proj, proj, x, w_a, w_b, w_c, w_o, gain.reshape(1, d))


def _xattn_kernel(x_ref, g_ref, wq_ref, kv_ref, wo_ref, pg_ref, o_ref, *, scale):
    x = x_ref[...]
    h = _rms(x, g_ref[...]).astype(BF16)
    q = (_dot(h, wq_ref[...]) * scale).astype(BF16)
    outs = []
    for hd in range(X_HEADS):
        cs = slice(hd * X_HEAD_DIM, (hd + 1) * X_HEAD_DIM)
        vs = slice(X_WIDTH + hd * X_HEAD_DIM, X_WIDTH + (hd + 1) * X_HEAD_DIM)
        z = _dot_nt(q[:, cs], kv_ref[:, cs])
        e = jnp.exp(z - jnp.max(z, axis=-1, keepdims=True))
        p = e / jnp.sum(e, axis=-1, keepdims=True)
        outs.append(_dot(p.astype(BF16), kv_ref[:, vs]))
    o = jnp.concatenate(outs, axis=-1).astype(BF16)
    o_ref[...] = x + _rms(_dot(o, wo_ref[...]), pg_ref[...])


def cross_attention(x, kv, seq, gain, w_q, w_o, post_gain, layer, *, bm):
    t, d = x.shape
    mem_len = kv.shape[0] // (t // seq)
    per_batch = seq // bm
    kernel = functools.partial(_xattn_kernel, scale=X_HEAD_DIM ** -0.5)
    return pl.pallas_call(
        kernel,
        grid=(t // bm,),
        in_specs=[
            pl.BlockSpec((bm, d), lambda i: (i, 0)),
            pl.BlockSpec((1, d), lambda i: (0, 0)),
            _resident((d, X_WIDTH), layer),
            pl.BlockSpec((mem_len, 2 * X_WIDTH), lambda i: (i // per_batch, 0)),
            _resident((X_WIDTH, d), layer),
            pl.BlockSpec((1, d), lambda i: (0, 0)),
        ],
        out_specs=pl.BlockSpec((bm, d), lambda i: (i, 0)),
        out_shape=jax.ShapeDtypeStruct((t, d), F32),
        compiler_params=_cparams("parallel"),
        name="cross_attention",
    )(x, gain.reshape(1, d), w_q, kv, w_o, post_gain.reshape(1, d))


def _ffn_kernel(x_ref, g_ref, wi_ref, wo_ref, pg_ref, o_ref, h_ref, *, nf):
    f = pl.program_id(1)

    @pl.when(f == 0)
    def _():
        h_ref[...] = _rms(x_ref[...], g_ref[...]).astype(BF16)
        o_ref[...] = jnp.zeros_like(o_ref)

    bf = wi_ref.shape[1] // 2
    nb = bf // LANES
    wi = jnp.concatenate(
        [wi_ref[:, half * bf + k * LANES:half * bf + (k + 1) * LANES] for k in range(nb) for half in (0, 1)],
        axis=1)
    gu = _dot(h_ref[...], wi)
    a = jnp.concatenate(
        [jax.nn.silu(gu[:, 2 * k * LANES:(2 * k + 1) * LANES]) * gu[:, (2 * k + 1) * LANES:(2 * k + 2) * LANES]
         for k in range(nb)], axis=1).astype(BF16)
    o_ref[...] += _dot(a, wo_ref[...])

    @pl.when(f == nf - 1)
    def _():
        o_ref[...] = x_ref[...] + _rms(o_ref[...], pg_ref[...])


def _ffn_in_tiles(w_in, bf):
    *lead, d, h2 = w_in.shape
    k = len(lead)
    nf = h2 // (2 * bf)
    w = w_in.reshape(*lead, d, 2, nf, bf).transpose(*range(k), k + 2, k, k + 1, k + 3)
    return w.reshape(*lead, nf, d, 2 * bf)


def ffn(x, gain, w_in_tiles, w_out, post_gain, layer, *, bm):
    t, d = x.shape
    _, nf, _, bf2 = w_in_tiles.shape
    bf = bf2 // 2
    kernel = functools.partial(_ffn_kernel, nf=nf)
    return pl.pallas_call(
        kernel,
        grid=(t // bm, nf),
        in_specs=[
            pl.BlockSpec((bm, d), lambda i, f: (i, 0)),
            pl.BlockSpec((1, d), lambda i, f: (0, 0)),
            pl.BlockSpec((None, None, d, bf2), lambda i, f: (layer, f, 0, 0)),
            pl.BlockSpec((None, bf, d), lambda i, f: (layer, f, 0)),
            pl.BlockSpec((1, d), lambda i, f: (0, 0)),
        ],
        out_specs=pl.BlockSpec((bm, d), lambda i, f: (i, 0)),
        out_shape=jax.ShapeDtypeStruct((t, d), F32),
        scratch_shapes=[pltpu.VMEM((bm, d), BF16)],
        compiler_params=_cparams("parallel", "arbitrary"),
        name="ffn",
    )(x, gain.reshape(1, d), w_in_tiles, w_out, post_gain.reshape(1, d))


def kernel(x, mem, mix_pre_gain, mix_post_gain, w_in, gm_v_gain, gm_w_s, gm_b_s, s5_lam_re, s5_lam_im, s5_log_dt, s5_b_re, s5_b_im, s5_c_re, s5_c_im, s5_d, s5_w_glu, w_br_a, w_br_b, w_br_c, w_out, xattn_pre_gain, xattn_post_gain, mem_gain, w_xq, w_xkv, w_xo, ffn_pre_gain, ffn_post_gain, w_ffn_in, w_ffn_out):
    batch, seq, d = x.shape
    depth = w_in.shape[0]
    t = batch * seq
    bm = math.gcd(seq, 512)
    xs = x.reshape(t, d)
    mem2 = mem.reshape(batch * mem.shape[1], d)
    n_in = w_in.shape[2]
    bn = n_in // 3 if n_in % (3 * LANES) == 0 else n_in
    assert d == D_MODEL and n_in == W_IN_GATE_COL + N_BRANCH * D_MODEL
    w_in_t = _col_tiles(jnp.concatenate([w_in[..., W_IN_GATE_COL:], w_in[..., :W_IN_GATE_COL]], axis=-1), bn).astype(BF16)
    w_xkv_t = _col_tiles(w_xkv, w_xkv.shape[-1]).astype(BF16)
    w_ffn_in_t = _ffn_in_tiles(w_ffn_in, 512).astype(BF16)
    w_ffn_out16, w_glu16, w_xq16, w_xo16 = (w.astype(BF16) for w in (w_ffn_out, s5_w_glu, w_xq, w_xo))
    w_a16, w_b16, w_c16, w_o16 = (w.astype(BF16) for w in (w_br_a, w_br_b, w_br_c, w_out))
    ts = math.gcd(seq, 256)
    wb, wc, pa, pk = jax.vmap(functools.partial(_s5_tables, seg=ts // SUBLANES))(
        s5_lam_re, s5_lam_im, s5_log_dt, s5_b_re, s5_b_im, s5_c_re, s5_c_im)
    for l in range(depth):
        proj = norm_matmul(xs, mix_pre_gain[l], w_in_t, l, bm=bm)
        y_a = sb_attention(proj, batch, seq, bq=math.gcd(seq, 256), bk=256, chains=4)
        y_b = spatial_gating(proj, gm_v_gain[l], gm_w_s[l], gm_b_s[l], bm=bm)
        y_c = s5_mixer(proj, batch, seq, wb, wc, pa, pk, s5_d, w_glu16, l, ts=ts)
        xs = merge_out(y_a, y_b, y_c, proj, xs, w_a16, w_b16, w_c16, w_o16, mix_post_gain[l], l, bm=bm)
        kv = norm_matmul(mem2, mem_gain[l], w_xkv_t, l, bm=math.gcd(mem2.shape[0], 256))
        xs = cross_attention(xs, kv, seq, xattn_pre_gain[l], w_xq16, w_xo16, xattn_post_gain[l], l, bm=bm)
        xs = ffn(xs, ffn_pre_gain[l], w_ffn_in_t, w_ffn_out16, ffn_post_gain[l], l, bm=bm)
    return xs.reshape(batch, seq, d)
```

```python
import functools
import math

import jax
import jax.numpy as jnp
from jax import lax
from jax.experimental import pallas as pl
from jax.experimental.pallas import tpu as pltpu

F32 = jnp.float32
BF16 = jnp.bfloat16

RMS_EPS = 1e-6

SB_HEADS = 4
SB_HEAD_DIM = 256
SB_WIDTH = SB_HEADS * SB_HEAD_DIM
GM_GROUPS = 4
GM_CHUNK = 128
GM_WIDTH = 512
S5_WIDTH = 512
S5_GROUP_DIM = 16
S5_GROUPS = S5_WIDTH // S5_GROUP_DIM
S5_STATE = 64
S5_LANES = S5_GROUPS * S5_STATE
X_HEADS = 4
X_HEAD_DIM = 128
X_WIDTH = X_HEADS * X_HEAD_DIM

D_MODEL = 2048
N_BRANCH = 3

W_IN_GATE_COL = 3 * SB_WIDTH + 2 * GM_WIDTH + S5_WIDTH
OFF_GATE = 0
OFF_Q = N_BRANCH * D_MODEL
OFF_K = OFF_Q + SB_WIDTH
OFF_V = OFF_K + SB_WIDTH
OFF_GM = OFF_V + SB_WIDTH
OFF_S5 = OFF_GM + 2 * GM_WIDTH

SUBLANES = 8
LANES = 128

EXP_ZERO_BELOW = -104.0
SB_DEAD_CARRY = -1e30

V7X_VMEM_BYTES = 64 * 1024 * 1024
VMEM_LIMIT = V7X_VMEM_BYTES - 4 * 1024 * 1024


def _cparams(*sem):
    return pltpu.CompilerParams(dimension_semantics=sem, vmem_limit_bytes=VMEM_LIMIT)


def _rms(xf, gain):
    ms = jnp.mean(xf * xf, axis=-1, keepdims=True)
    return xf * lax.rsqrt(ms + RMS_EPS) * gain


def _dot(a, b):
    return jnp.dot(a, b, preferred_element_type=F32)


def _dot_nt(a, b):
    return lax.dot_general(a, b, (((1,), (1,)), ((), ())), preferred_element_type=F32)


def _resident(shape, layer=None):
    if layer is None:
        return pl.BlockSpec(shape, lambda *_: (0,) * len(shape), pipeline_mode=pl.Buffered(1))
    return pl.BlockSpec((None,) + tuple(shape), lambda *_: (layer,) + (0,) * len(shape),
                        pipeline_mode=pl.Buffered(1))


def _norm_matmul_kernel(x_ref, g_ref, w_ref, o_ref, h_ref):
    @pl.when(pl.program_id(1) == 0)
    def _():
        h_ref[...] = _rms(x_ref[...], g_ref[...]).astype(BF16)

    o_ref[...] = _dot(h_ref[...], w_ref[...]).astype(o_ref.dtype)


def norm_matmul(x, gain, w, layer, *, bm, bn):
    t, d = x.shape
    n = w.shape[2]
    return pl.pallas_call(
        _norm_matmul_kernel,
        grid=(t // bm, n // bn),
        in_specs=[
            pl.BlockSpec((bm, d), lambda i, j: (i, 0)),
            pl.BlockSpec((1, d), lambda i, j: (0, 0)),
            pl.BlockSpec((None, d, bn), lambda i, j: (layer, 0, j)),
        ],
        out_specs=pl.BlockSpec((bm, bn), lambda i, j: (i, j)),
        out_shape=jax.ShapeDtypeStruct((t, n), BF16),
        scratch_shapes=[pltpu.VMEM((bm, d), BF16)],
        compiler_params=_cparams("parallel", "arbitrary"),
        name="norm_matmul",
    )(x, gain.reshape(1, d), w)


def _sb_tile(z, mask, carry, v, uo, bk):
    sp = jnp.log(1.0 + jnp.exp(-jnp.abs(z)))
    ls_neg = jnp.minimum(-z, 0.0) - sp
    ls_pos = z + ls_neg
    if mask is not None:
        ls_neg = jnp.where(mask, ls_neg, 0.0)
    hi = ls_neg.astype(BF16)
    lo = (ls_neg - hi.astype(F32)).astype(BF16)
    cs = _dot(hi, uo) + _dot(lo, uo)
    log_w = ls_pos + cs[:, :bk] + jnp.concatenate([carry] * (bk // LANES), axis=1)
    w = jnp.exp(log_w)
    if mask is not None:
        w = jnp.where(mask, w, 0.0)
    return _dot(w.astype(BF16), v), carry + cs[:, bk:]


def _sb_attn_kernel(q_ref, k_ref, v_ref, uo_ref, o_ref, acc_ref, carry_ref, *, bq, bk, chains, scale):
    i = pl.program_id(2)
    uo = uo_ref[...]
    nd = bq // bk
    acc_ref[...] = jnp.zeros_like(acc_ref)
    carry_ref[...] = jnp.zeros_like(carry_ref)

    for c in range(chains):
        for d in reversed(range(nd)):
            rows = slice(c * bq + d * bk, (c + 1) * bq)
            m = bq - d * bk
            off = pl.multiple_of((i * chains + c) * bq + d * bk, bk)
            z = _dot_nt(q_ref[rows, :], k_ref[pl.ds(off, bk), :]) * scale
            row = lax.broadcasted_iota(jnp.int32, (m, bk), 0)
            col = lax.broadcasted_iota(jnp.int32, (m, bk), 1)
            pv, carry = _sb_tile(z, col < row, carry_ref[rows, :], v_ref[pl.ds(off, bk), :], uo, bk)
            acc_ref[rows, :] += pv
            carry_ref[rows, :] = carry

    first = [(i * chains + c) * nd - 1 for c in range(chains)]

    def cond(state):
        t, lives = state
        live = [jnp.logical_and(first[c] - t >= 0, lives[c] > EXP_ZERO_BELOW) for c in range(chains)]
        return functools.reduce(jnp.logical_or, live)

    def body(state):
        t, _ = state
        pvs, lives = [], []
        for c in range(chains):
            rows = slice(c * bq, (c + 1) * bq)
            kb = first[c] - t
            off = pl.multiple_of(jnp.maximum(kb, 0) * bk, bk)
            carry_in = carry_ref[rows, :] + jnp.where(kb < 0, SB_DEAD_CARRY, 0.0)
            z = _dot_nt(q_ref[rows, :], k_ref[pl.ds(off, bk), :]) * scale
            pv, carry = _sb_tile(z, None, carry_in, v_ref[pl.ds(off, bk), :], uo, bk)
            carry_ref[rows, :] = carry
            pvs.append(pv)
            lives.append(jnp.max(carry))
        acc_ref[...] += jnp.concatenate(pvs, axis=0)
        return t + 1, tuple(lives)

    lax.while_loop(cond, body, body((jnp.int32(0), None)))
    o_ref[...] = acc_ref[...].astype(o_ref.dtype)


def sb_attention(proj, batch, seq, *, bq, bk=LANES, chains=1):
    t = proj.shape[0]
    nq = seq // (bq * chains)
    bq_all = bq * chains
    dh = SB_HEAD_DIM
    r = lax.broadcasted_iota(jnp.int32, (bk, bk + LANES), 0)
    c = lax.broadcasted_iota(jnp.int32, (bk, bk + LANES), 1)
    uo = jnp.where(jnp.logical_or(r > c, c >= bk), 1.0, 0.0).astype(BF16)
    kernel = functools.partial(_sb_attn_kernel, bq=bq, bk=bk, chains=chains, scale=dh ** -0.5)
    return pl.pallas_call(
        kernel,
        grid=(batch, SB_HEADS, nq),
        in_specs=[
            pl.BlockSpec((bq_all, dh), lambda b, h, i: (b * nq + i, OFF_Q // dh + h)),
            pl.BlockSpec((seq, dh), lambda b, h, i: (b, OFF_K // dh + h)),
            pl.BlockSpec((seq, dh), lambda b, h, i: (b, OFF_V // dh + h)),
            pl.BlockSpec((bk, bk + LANES), lambda b, h, i: (0, 0)),
        ],
        out_specs=pl.BlockSpec((bq_all, dh), lambda b, h, i: (b * nq + i, h)),
        out_shape=jax.ShapeDtypeStruct((t, SB_WIDTH), BF16),
        scratch_shapes=[pltpu.VMEM((bq_all, dh), F32), pltpu.VMEM((bq_all, LANES), F32)],
        compiler_params=_cparams("parallel", "parallel", "arbitrary"),
        name="sb_attention",
    )(proj, proj, proj, uo)


def _gm_kernel(uv_ref, gain_ref, w_ref, b_ref, o_ref, *, chunks):
    uv = jax.nn.gelu(uv_ref[...].astype(F32))
    u = uv[:, :GM_WIDTH]
    v = _rms(uv[:, GM_WIDTH:], gain_ref[...]).astype(BF16)
    row = lax.broadcasted_iota(jnp.int32, (GM_CHUNK, GM_CHUNK), 0)
    col = lax.broadcasted_iota(jnp.int32, (GM_CHUNK, GM_CHUNK), 1)
    gd = GM_WIDTH // GM_GROUPS
    for g in range(GM_GROUPS):
        wg = jnp.where(col <= row, w_ref[g], 0.0).astype(BF16)
        bg = b_ref[g]
        for c in range(chunks):
            rs = slice(c * GM_CHUNK, (c + 1) * GM_CHUNK)
            cs = slice(g * gd, (g + 1) * gd)
            mixed = _dot(wg, v[rs, cs]) + bg
            o_ref[rs, cs] = (u[rs, cs] * mixed).astype(o_ref.dtype)


def spatial_gating(proj, v_gain, w_s, b_s, *, bm):
    t = proj.shape[0]
    gd = GM_WIDTH // GM_GROUPS
    b_full = jnp.broadcast_to(b_s[:, :, None], (GM_GROUPS, GM_CHUNK, gd)).astype(F32)
    kernel = functools.partial(_gm_kernel, chunks=bm // GM_CHUNK)
    return pl.pallas_call(
        kernel,
        grid=(t // bm,),
        in_specs=[
            pl.BlockSpec((bm, 2 * GM_WIDTH), lambda i: (i, OFF_GM // (2 * GM_WIDTH))),
            pl.BlockSpec((1, GM_WIDTH), lambda i: (0, 0)),
            pl.BlockSpec((GM_GROUPS, GM_CHUNK, GM_CHUNK), lambda i: (0, 0, 0)),
            pl.BlockSpec((GM_GROUPS, GM_CHUNK, gd), lambda i: (0, 0, 0)),
        ],
        out_specs=pl.BlockSpec((bm, GM_WIDTH), lambda i: (i, 0)),
        out_shape=jax.ShapeDtypeStruct((t, GM_WIDTH), BF16),
        compiler_params=_cparams("parallel"),
        name="spatial_gating",
    )(proj, v_gain.reshape(1, GM_WIDTH), w_s, b_full)


S5_LANE_GROUP = 4
S5_UNROLL = 32


def _s5_lane_chunks():
    n = S5_LANES
    per_half = n // (2 * LANES)
    out = []
    for lc in range(n // LANES):
        hh, k = divmod(lc, per_half)
        out.append((pl.ds(hh * n + k * LANES, LANES), pl.ds(hh * n + n // 2 + k * LANES, LANES)))
    return out


def _cmul_add(pr, pi, sr, si, br, bi):
    return br + (pr * sr - pi * si), bi + (pr * si + pi * sr)


def _s5_kernel(u_ref, perm_ref, permt_ref, wb_ref, pa_ref, pk_ref, wc_ref, d_ref, wg_ref, o_ref,
               x_ref, carry_ref, *, ts):
    n = S5_LANES
    seg = ts // SUBLANES
    chunks = _s5_lane_chunks()

    @pl.when(pl.program_id(1) == 0)
    def _():
        carry_ref[...] = jnp.zeros_like(carry_ref)

    u = _dot(perm_ref[...], u_ref[...]).astype(BF16)
    hw = S5_WIDTH // 2
    for hh in range(2):
        x_ref[:, hh * n:(hh + 1) * n] = _dot(u[:, hh * hw:(hh + 1) * hw], wb_ref[hh])

    row = lax.broadcasted_iota(jnp.int32, (SUBLANES, LANES), 0)
    zero = jnp.zeros((SUBLANES, LANES), F32)
    for g0 in range(0, len(chunks), S5_LANE_GROUP):
        group = chunks[g0:g0 + S5_LANE_GROUP]
        a = [(pa_ref[0, :, re], pa_ref[0, :, im]) for re, im in group]

        def step(k, s, group=group, a=a):
            rows = pl.ds(pl.multiple_of(k * SUBLANES, SUBLANES), SUBLANES)
            out = []
            for (re, im), (ar, ai), (sr, si) in zip(group, a, s):
                nr, ni = _cmul_add(ar, ai, sr, si, x_ref[rows, re], x_ref[rows, im])
                x_ref[rows, re] = nr
                x_ref[rows, im] = ni
                out.append((nr, ni))
            return tuple(out)

        ends = lax.fori_loop(0, seg, step, tuple((zero, zero) for _ in group), unroll=S5_UNROLL)

        enter = []
        for (re, im), (er, ei) in zip(group, ends):
            for tab, shift in ((1, 1), (2, 2), (3, 4)):
                er, ei = _cmul_add(pa_ref[tab, :, re], pa_ref[tab, :, im],
                                   pltpu.roll(er, shift, 0), pltpu.roll(ei, shift, 0), er, ei)
            cr, ci = carry_ref[:, re], carry_ref[:, im]
            er, ei = _cmul_add(pa_ref[4, :, re], pa_ref[4, :, im], cr, ci, er, ei)
            enter.append((jnp.where(row == 0, cr, pltpu.roll(er, 1, 0)),
                          jnp.where(row == 0, ci, pltpu.roll(ei, 1, 0))))
            carry_ref[:, re] = jnp.broadcast_to(er[SUBLANES - 1:, :], (SUBLANES, LANES))
            carry_ref[:, im] = jnp.broadcast_to(ei[SUBLANES - 1:, :], (SUBLANES, LANES))

        def fix(k, c, group=group, enter=enter):
            rows = pl.ds(pl.multiple_of(k * SUBLANES, SUBLANES), SUBLANES)
            for (re, im), (sr, si) in zip(group, enter):
                nr, ni = _cmul_add(pk_ref[k, :, re], pk_ref[k, :, im], sr, si, x_ref[rows, re], x_ref[rows, im])
                x_ref[rows, re] = nr
                x_ref[rows, im] = ni
            return c

        lax.fori_loop(0, seg, fix, 0, unroll=S5_UNROLL)

    y = jnp.concatenate([_dot(x_ref[:, hh * n:(hh + 1) * n].astype(BF16), wc_ref[hh]) for hh in range(2)], axis=-1)
    y = jax.nn.gelu(y + d_ref[...] * u.astype(F32)).astype(BF16)
    ag = _dot(y, wg_ref[...])
    out = (ag[:, :S5_WIDTH] * jax.nn.sigmoid(ag[:, S5_WIDTH:])).astype(BF16)
    o_ref[...] = _dot(permt_ref[...], out).astype(o_ref.dtype)


def _s5_tables(lam_re, lam_im, log_dt, b_re, b_im, c_re, c_im, seg):
    g, n, c = S5_GROUPS, S5_STATE, S5_GROUP_DIM
    lr, li = lam_re.astype(F32), lam_im.astype(F32)
    dt = jnp.exp(log_dt.astype(F32))[:, None]
    mag = jnp.exp(lr * dt)
    a_re, a_im = mag * jnp.cos(li * dt), mag * jnp.sin(li * dt)
    den = lr * lr + li * li
    x_ = a_re - 1.0
    f_re = (x_ * lr + a_im * li) / den
    f_im = (a_im * lr - x_ * li) / den
    br, bi = b_re.astype(F32), b_im.astype(F32)
    bb_re = f_re[..., None] * br - f_im[..., None] * bi
    bb_im = f_re[..., None] * bi + f_im[..., None] * br
    gh = g // 2
    eye = jnp.eye(gh, dtype=F32)

    def in_map(bb):
        return jnp.einsum('hgnc,gk->hgckn', bb.reshape(2, gh, n, c), eye).reshape(2, gh * c, gh * n)

    def out_map(cc):
        return jnp.einsum('hgcn,gk->hgnkc', cc.astype(F32).reshape(2, gh, c, n), eye).reshape(2, gh * n, gh * c)

    wb = jnp.concatenate([in_map(bb_re), in_map(bb_im)], axis=2).astype(BF16)
    wc = jnp.concatenate([out_map(c_re), -out_map(c_im)], axis=1).astype(BF16)

    def lanes(pr, pi):
        hl = gh * n
        return jnp.concatenate([pr[..., :hl], pi[..., :hl], pr[..., hl:], pi[..., hl:]], axis=-1)

    exps = list(range(1, seg + 1)) + [seg * m for m in (1, 2, 4)] + [seg * (j + 1) for j in range(SUBLANES)]
    sq = [(a_re.reshape(1, -1), a_im.reshape(1, -1))]
    for _ in range(max(exps).bit_length() - 1):
        sr, si = sq[-1]
        sq.append((sr * sr - si * si, 2.0 * sr * si))
    e = jnp.asarray(exps, jnp.int32)[:, None]
    pr, pi = jnp.ones((len(exps), g * n), F32), jnp.zeros((len(exps), g * n), F32)
    for bit, (sr, si) in enumerate(sq):
        on = ((e >> bit) & 1) == 1
        pr, pi = jnp.where(on, pr * sr - pi * si, pr), jnp.where(on, pr * si + pi * sr, pi)
    pw = lanes(pr, pi)
    rows = jnp.arange(SUBLANES)[:, None]
    pa = [jnp.broadcast_to(pw[0][None], (SUBLANES, pw.shape[-1]))]
    for m, shift in enumerate((1, 2, 4)):
        pa.append(jnp.where(rows >= shift, pw[seg + m][None], 0.0))
    pa.append(pw[seg + 3:])
    pk = jnp.broadcast_to(pw[:seg, None, :], (seg, SUBLANES, pw.shape[-1]))
    return wb, wc, jnp.stack(pa).astype(F32), pk.astype(F32)


def s5_mixer(proj, batch, seq, wb, wc, pa, pk, d_skip, w_glu, layer, *, ts):
    t = proj.shape[0]
    nt = seq // ts
    n2 = 2 * S5_LANES
    seg = ts // SUBLANES
    new = jnp.arange(ts)
    perm = (new[:, None] % SUBLANES * seg + new[:, None] // SUBLANES == new[None, :]).astype(BF16)
    kernel = functools.partial(_s5_kernel, ts=ts)
    return pl.pallas_call(
        kernel,
        grid=(batch, nt),
        in_specs=[
            pl.BlockSpec((ts, S5_WIDTH), lambda b, i: (b * nt + i, OFF_S5 // S5_WIDTH)),
            _resident((ts, ts)),
            _resident((ts, ts)),
            _resident((2, S5_WIDTH // 2, S5_LANES), layer),
            _resident((5, SUBLANES, n2), layer),
            _resident((seg, SUBLANES, n2), layer),
            _resident((2, S5_LANES, S5_WIDTH // 2), layer),
            _resident((1, S5_WIDTH), layer),
            _resident((S5_WIDTH, 2 * S5_WIDTH), layer),
        ],
        out_specs=pl.BlockSpec((ts, S5_WIDTH), lambda b, i: (b * nt + i, 0)),
        out_shape=jax.ShapeDtypeStruct((t, S5_WIDTH), BF16),
        scratch_shapes=[pltpu.VMEM((ts, n2), F32), pltpu.VMEM((SUBLANES, n2), F32)],
        compiler_params=_cparams("parallel", "arbitrary"),
        name="s5_mixer",
    )(proj, perm, perm.T, wb, pa, pk, wc, d_skip.reshape(-1, 1, S5_WIDTH).astype(F32), w_glu)


def _merge_kernel(ya_ref, yb_ref, yc_ref, g0_ref, g1_ref, g2_ref, x_ref, wa_ref, wb_ref, wc_ref,
                  wo_ref, gain_ref, o_ref, m_ref, *, cw):
    ya, yb, yc = ya_ref[...], yb_ref[...], yc_ref[...]
    for k in range(m_ref.shape[1] // cw):
        cs = slice(k * cw, (k + 1) * cw)
        m = (jax.nn.sigmoid(g0_ref[:, cs].astype(F32)) * _dot(ya, wa_ref[:, cs])
             + jax.nn.sigmoid(g1_ref[:, cs].astype(F32)) * _dot(yb, wb_ref[:, cs])
             + jax.nn.sigmoid(g2_ref[:, cs].astype(F32)) * _dot(yc, wc_ref[:, cs]))
        m_ref[:, cs] = m.astype(BF16)
    y = _dot(m_ref[...], wo_ref[...])
    o_ref[...] = x_ref[...] + _rms(y, gain_ref[...])


def merge_out(y_a, y_b, y_c, proj, x, w_a, w_b, w_c, w_o, gain, layer, *, bm, cw=256):
    t, d = x.shape
    g0 = OFF_GATE // d
    kernel = functools.partial(_merge_kernel, cw=cw)
    return pl.pallas_call(
        kernel,
        grid=(t // bm,),
        in_specs=[
            pl.BlockSpec((bm, SB_WIDTH), lambda i: (i, 0)),
            pl.BlockSpec((bm, GM_WIDTH), lambda i: (i, 0)),
            pl.BlockSpec((bm, S5_WIDTH), lambda i: (i, 0)),
            pl.BlockSpec((bm, d), lambda i: (i, g0)),
            pl.BlockSpec((bm, d), lambda i: (i, g0 + 1)),
            pl.BlockSpec((bm, d), lambda i: (i, g0 + 2)),
            pl.BlockSpec((bm, d), lambda i: (i, 0)),
            _resident((SB_WIDTH, d), layer),
            _resident((GM_WIDTH, d), layer),
            _resident((S5_WIDTH, d), layer),
            _resident((d, d), layer),
            _resident((1, d)),
        ],
        out_specs=pl.BlockSpec((bm, d), lambda i: (i, 0)),
        out_shape=jax.ShapeDtypeStruct((t, d), F32),
        scratch_shapes=[pltpu.VMEM((bm, d), BF16)],
        compiler_params=_cparams("parallel"),
        name="merge_out",
    )(y_a, y_b, y_c, proj, proj, proj, x, w_a, w_b, w_c, w_o, gain.reshape(1, d))


def _xattn_kernel(x_ref, g_ref, wq_ref, kv_ref, wo_ref, pg_ref, o_ref, *, scale):
    x = x_ref[...]
    h = _rms(x, g_ref[...]).astype(BF16)
    q = (_dot(h, wq_ref[...]) * scale).astype(BF16)
    outs = []
    for hd in range(X_HEADS):
        cs = slice(hd * X_HEAD_DIM, (hd + 1) * X_HEAD_DIM)
        vs = slice(X_WIDTH + hd * X_HEAD_DIM, X_WIDTH + (hd + 1) * X_HEAD_DIM)
        z = _dot_nt(q[:, cs], kv_ref[:, cs])
        e = jnp.exp(z - jnp.max(z, axis=-1, keepdims=True))
        p = e / jnp.sum(e, axis=-1, keepdims=True)
        outs.append(_dot(p.astype(BF16), kv_ref[:, vs]))
    o = jnp.concatenate(outs, axis=-1).astype(BF16)
    o_ref[...] = x + _rms(_dot(o, wo_ref[...]), pg_ref[...])


def cross_attention(x, kv, seq, gain, w_q, w_o, post_gain, layer, *, bm):
    t, d = x.shape
    mem_len = kv.shape[0] // (t // seq)
    per_batch = seq // bm
    kernel = functools.partial(_xattn_kernel, scale=X_HEAD_DIM ** -0.5)
    return pl.pallas_call(
        kernel,
        grid=(t // bm,),
        in_specs=[
            pl.BlockSpec((bm, d), lambda i: (i, 0)),
            pl.BlockSpec((1, d), lambda i: (0, 0)),
            _resident((d, X_WIDTH), layer),
            pl.BlockSpec((mem_len, 2 * X_WIDTH), lambda i: (i // per_batch, 0)),
            _resident((X_WIDTH, d), layer),
            pl.BlockSpec((1, d), lambda i: (0, 0)),
        ],
        out_specs=pl.BlockSpec((bm, d), lambda i: (i, 0)),
        out_shape=jax.ShapeDtypeStruct((t, d), F32),
        compiler_params=_cparams("parallel"),
        name="cross_attention",
    )(x, gain.reshape(1, d), w_q, kv, w_o, post_gain.reshape(1, d))


def _ffn_kernel(x_ref, g_ref, wg_ref, wu_ref, wo_ref, pg_ref, o_ref, h_ref, *, nf):
    f = pl.program_id(1)

    @pl.when(f == 0)
    def _():
        h_ref[...] = _rms(x_ref[...], g_ref[...]).astype(BF16)
        o_ref[...] = jnp.zeros_like(o_ref)

    nb = wg_ref.shape[1] // LANES
    wi = jnp.concatenate(
        [w[:, k * LANES:(k + 1) * LANES] for k in range(nb) for w in (wg_ref, wu_ref)], axis=1)
    gu = _dot(h_ref[...], wi)
    a = jnp.concatenate(
        [jax.nn.silu(gu[:, 2 * k * LANES:(2 * k + 1) * LANES]) * gu[:, (2 * k + 1) * LANES:(2 * k + 2) * LANES]
         for k in range(nb)], axis=1).astype(BF16)
    o_ref[...] += _dot(a, wo_ref[...])

    @pl.when(f == nf - 1)
    def _():
        o_ref[...] = x_ref[...] + _rms(o_ref[...], pg_ref[...])


def ffn(x, gain, w_in, w_out, post_gain, layer, *, bm, bf):
    t, d = x.shape
    nf = w_out.shape[1] // bf
    kernel = functools.partial(_ffn_kernel, nf=nf)
    return pl.pallas_call(
        kernel,
        grid=(t // bm, nf),
        in_specs=[
            pl.BlockSpec((bm, d), lambda i, f: (i, 0)),
            pl.BlockSpec((1, d), lambda i, f: (0, 0)),
            pl.BlockSpec((None, d, bf), lambda i, f: (layer, 0, f)),
            pl.BlockSpec((None, d, bf), lambda i, f: (layer, 0, nf + f)),
            pl.BlockSpec((None, bf, d), lambda i, f: (layer, f, 0)),
            pl.BlockSpec((1, d), lambda i, f: (0, 0)),
        ],
        out_specs=pl.BlockSpec((bm, d), lambda i, f: (i, 0)),
        out_shape=jax.ShapeDtypeStruct((t, d), F32),
        scratch_shapes=[pltpu.VMEM((bm, d), BF16)],
        compiler_params=_cparams("parallel", "arbitrary"),
        name="ffn",
    )(x, gain.reshape(1, d), w_in, w_in, w_out, post_gain.reshape(1, d))


def kernel(x, mem, mix_pre_gain, mix_post_gain, w_in, gm_v_gain, gm_w_s, gm_b_s, s5_lam_re, s5_lam_im, s5_log_dt, s5_b_re, s5_b_im, s5_c_re, s5_c_im, s5_d, s5_w_glu, w_br_a, w_br_b, w_br_c, w_out, xattn_pre_gain, xattn_post_gain, mem_gain, w_xq, w_xkv, w_xo, ffn_pre_gain, ffn_post_gain, w_ffn_in, w_ffn_out):
    batch, seq, d = x.shape
    depth = w_in.shape[0]
    t = batch * seq
    bm = math.gcd(seq, 512)
    xs = x.reshape(t, d)
    mem2 = mem.reshape(batch * mem.shape[1], d)
    n_in = w_in.shape[2]
    bn = n_in // 3 if n_in % (3 * LANES) == 0 else n_in
    assert d == D_MODEL and n_in == W_IN_GATE_COL + N_BRANCH * D_MODEL
    w_in16 = jnp.concatenate([w_in[..., W_IN_GATE_COL:], w_in[..., :W_IN_GATE_COL]], axis=-1).astype(BF16)
    w_ffn_in16, w_ffn_out16, w_glu16 = (w.astype(BF16) for w in (w_ffn_in, w_ffn_out, s5_w_glu))
    w_xq16, w_xkv16, w_xo16 = (w.astype(BF16) for w in (w_xq, w_xkv, w_xo))
    w_a16, w_b16, w_c16, w_o16 = (w.astype(BF16) for w in (w_br_a, w_br_b, w_br_c, w_out))
    ts = math.gcd(seq, 256)
    wb, wc, pa, pk = jax.vmap(functools.partial(_s5_tables, seg=ts // SUBLANES))(
        s5_lam_re, s5_lam_im, s5_log_dt, s5_b_re, s5_b_im, s5_c_re, s5_c_im)
    for l in range(depth):
        proj = norm_matmul(xs, mix_pre_gain[l], w_in16, l, bm=bm, bn=bn)
        y_a = sb_attention(proj, batch, seq, bq=math.gcd(seq, 256), bk=256, chains=4)
        y_b = spatial_gating(proj, gm_v_gain[l], gm_w_s[l], gm_b_s[l], bm=bm)
        y_c = s5_mixer(proj, batch, seq, wb, wc, pa, pk, s5_d, w_glu16, l, ts=ts)
        xs = merge_out(y_a, y_b, y_c, proj, xs, w_a16, w_b16, w_c16, w_o16, mix_post_gain[l], l, bm=bm)
        kv = norm_matmul(mem2, mem_gain[l], w_xkv16, l, bm=math.gcd(mem2.shape[0], 256), bn=2 * X_WIDTH)
        xs = cross_attention(xs, kv, seq, xattn_pre_gain[l], w_xq16, w_xo16, xattn_post_gain[l], l, bm=bm)
        xs = ffn(xs, ffn_pre_gain[l], w_ffn_in16, w_ffn_out16, ffn_post_gain[l], l, bm=bm, bf=512)
    return xs.reshape(batch, seq, d)
```

```python
import functools
import math

import jax
import jax.numpy as jnp
from jax import lax
from jax.experimental import pallas as pl
from jax.experimental.pallas import tpu as pltpu

F32 = jnp.float32
BF16 = jnp.bfloat16

RMS_EPS = 1e-6

SB_HEADS = 4
SB_HEAD_DIM = 256
SB_WIDTH = SB_HEADS * SB_HEAD_DIM
GM_GROUPS = 4
GM_CHUNK = 128
GM_WIDTH = 512
S5_WIDTH = 512
S5_GROUP_DIM = 16
S5_GROUPS = S5_WIDTH // S5_GROUP_DIM
S5_STATE = 64
S5_LANES = S5_GROUPS * S5_STATE
X_HEADS = 4
X_HEAD_DIM = 128
X_WIDTH = X_HEADS * X_HEAD_DIM

D_MODEL = 2048
N_BRANCH = 3

W_IN_GATE_COL = 3 * SB_WIDTH + 2 * GM_WIDTH + S5_WIDTH
OFF_GATE = 0
OFF_Q = N_BRANCH * D_MODEL
OFF_K = OFF_Q + SB_WIDTH
OFF_V = OFF_K + SB_WIDTH
OFF_GM = OFF_V + SB_WIDTH
OFF_S5 = OFF_GM + 2 * GM_WIDTH

SUBLANES = 8
LANES = 128

EXP_ZERO_BELOW = -104.0
SB_DEAD_CARRY = -1e30

V7X_VMEM_BYTES = 64 * 1024 * 1024
VMEM_LIMIT = V7X_VMEM_BYTES - 4 * 1024 * 1024


def _cparams(*sem):
    return pltpu.CompilerParams(dimension_semantics=sem, vmem_limit_bytes=VMEM_LIMIT)


def _rms(xf, gain):
    ms = jnp.mean(xf * xf, axis=-1, keepdims=True)
    return xf * lax.rsqrt(ms + RMS_EPS) * gain


def _dot(a, b):
    return jnp.dot(a, b, preferred_element_type=F32)


def _dot_nt(a, b):
    return lax.dot_general(a, b, (((1,), (1,)), ((), ())), preferred_element_type=F32)


def _resident(shape, layer=None):
    if layer is None:
        return pl.BlockSpec(shape, lambda *_: (0,) * len(shape), pipeline_mode=pl.Buffered(1))
    return pl.BlockSpec((None,) + tuple(shape), lambda *_: (layer,) + (0,) * len(shape),
                        pipeline_mode=pl.Buffered(1))


def _norm_matmul_kernel(x_ref, g_ref, w_ref, o_ref, h_ref):
    @pl.when(pl.program_id(1) == 0)
    def _():
        h_ref[...] = _rms(x_ref[...], g_ref[...]).astype(BF16)

    o_ref[...] = _dot(h_ref[...], w_ref[...]).astype(o_ref.dtype)


def norm_matmul(x, gain, w, layer, *, bm, bn):
    t, d = x.shape
    n = w.shape[2]
    return pl.pallas_call(
        _norm_matmul_kernel,
        grid=(t // bm, n // bn),
        in_specs=[
            pl.BlockSpec((bm, d), lambda i, j: (i, 0)),
            pl.BlockSpec((1, d), lambda i, j: (0, 0)),
            pl.BlockSpec((None, d, bn), lambda i, j: (layer, 0, j)),
        ],
        out_specs=pl.BlockSpec((bm, bn), lambda i, j: (i, j)),
        out_shape=jax.ShapeDtypeStruct((t, n), BF16),
        scratch_shapes=[pltpu.VMEM((bm, d), BF16)],
        compiler_params=_cparams("parallel", "arbitrary"),
        name="norm_matmul",
    )(x, gain.reshape(1, d), w)


def _sb_tile(z, mask, carry, v, uo, bk):
    sp = jnp.log(1.0 + jnp.exp(-jnp.abs(z)))
    ls_neg = jnp.minimum(-z, 0.0) - sp
    ls_pos = z + ls_neg
    if mask is not None:
        ls_neg = jnp.where(mask, ls_neg, 0.0)
    hi = ls_neg.astype(BF16)
    lo = (ls_neg - hi.astype(F32)).astype(BF16)
    cs = _dot(hi, uo) + _dot(lo, uo)
    log_w = ls_pos + cs[:, :bk] + jnp.concatenate([carry] * (bk // LANES), axis=1)
    w = jnp.exp(log_w)
    if mask is not None:
        w = jnp.where(mask, w, 0.0)
    return _dot(w.astype(BF16), v), carry + cs[:, bk:]


def _sb_attn_kernel(q_ref, k_ref, v_ref, uo_ref, o_ref, acc_ref, carry_ref, *, bq, bk, chains, scale):
    i = pl.program_id(2)
    uo = uo_ref[...]
    nd = bq // bk
    acc_ref[...] = jnp.zeros_like(acc_ref)
    carry_ref[...] = jnp.zeros_like(carry_ref)

    for c in range(chains):
        for d in reversed(range(nd)):
            rows = slice(c * bq + d * bk, (c + 1) * bq)
            m = bq - d * bk
            off = pl.multiple_of((i * chains + c) * bq + d * bk, bk)
            z = _dot_nt(q_ref[rows, :], k_ref[pl.ds(off, bk), :]) * scale
            row = lax.broadcasted_iota(jnp.int32, (m, bk), 0)
            col = lax.broadcasted_iota(jnp.int32, (m, bk), 1)
            pv, carry = _sb_tile(z, col < row, carry_ref[rows, :], v_ref[pl.ds(off, bk), :], uo, bk)
            acc_ref[rows, :] += pv
            carry_ref[rows, :] = carry

    first = [(i * chains + c) * nd - 1 for c in range(chains)]

    def cond(state):
        t, lives = state
        live = [jnp.logical_and(first[c] - t >= 0, lives[c] > EXP_ZERO_BELOW) for c in range(chains)]
        return functools.reduce(jnp.logical_or, live)

    def body(state):
        t, _ = state
        pvs, lives = [], []
        for c in range(chains):
            rows = slice(c * bq, (c + 1) * bq)
            kb = first[c] - t
            off = pl.multiple_of(jnp.maximum(kb, 0) * bk, bk)
            carry_in = carry_ref[rows, :] + jnp.where(kb < 0, SB_DEAD_CARRY, 0.0)
            z = _dot_nt(q_ref[rows, :], k_ref[pl.ds(off, bk), :]) * scale
            pv, carry = _sb_tile(z, None, carry_in, v_ref[pl.ds(off, bk), :], uo, bk)
            carry_ref[rows, :] = carry
            pvs.append(pv)
            lives.append(jnp.max(carry))
        acc_ref[...] += jnp.concatenate(pvs, axis=0)
        return t + 1, tuple(lives)

    lax.while_loop(cond, body, body((jnp.int32(0), None)))
    o_ref[...] = acc_ref[...].astype(o_ref.dtype)


def sb_attention(proj, batch, seq, *, bq, bk=LANES, chains=1):
    t = proj.shape[0]
    nq = seq // (bq * chains)
    bq_all = bq * chains
    dh = SB_HEAD_DIM
    r = lax.broadcasted_iota(jnp.int32, (bk, bk + LANES), 0)
    c = lax.broadcasted_iota(jnp.int32, (bk, bk + LANES), 1)
    uo = jnp.where(jnp.logical_or(r > c, c >= bk), 1.0, 0.0).astype(BF16)
    kernel = functools.partial(_sb_attn_kernel, bq=bq, bk=bk, chains=chains, scale=dh ** -0.5)
    return pl.pallas_call(
        kernel,
        grid=(batch, SB_HEADS, nq),
        in_specs=[
            pl.BlockSpec((bq_all, dh), lambda b, h, i: (b * nq + i, OFF_Q // dh + h)),
            pl.BlockSpec((seq, dh), lambda b, h, i: (b, OFF_K // dh + h)),
            pl.BlockSpec((seq, dh), lambda b, h, i: (b, OFF_V // dh + h)),
            pl.BlockSpec((bk, bk + LANES), lambda b, h, i: (0, 0)),
        ],
        out_specs=pl.BlockSpec((bq_all, dh), lambda b, h, i: (b * nq + i, h)),
        out_shape=jax.ShapeDtypeStruct((t, SB_WIDTH), BF16),
        scratch_shapes=[pltpu.VMEM((bq_all, dh), F32), pltpu.VMEM((bq_all, LANES), F32)],
        compiler_params=_cparams("parallel", "parallel", "arbitrary"),
        name="sb_attention",
    )(proj, proj, proj, uo)


def _gm_kernel(uv_ref, gain_ref, w_ref, b_ref, o_ref, *, chunks):
    uv = jax.nn.gelu(uv_ref[...].astype(F32))
    u = uv[:, :GM_WIDTH]
    v = _rms(uv[:, GM_WIDTH:], gain_ref[...]).astype(BF16)
    row = lax.broadcasted_iota(jnp.int32, (GM_CHUNK, GM_CHUNK), 0)
    col = lax.broadcasted_iota(jnp.int32, (GM_CHUNK, GM_CHUNK), 1)
    gd = GM_WIDTH // GM_GROUPS
    for g in range(GM_GROUPS):
        wg = jnp.where(col <= row, w_ref[g], 0.0).astype(BF16)
        bg = b_ref[g]
        for c in range(chunks):
            rs = slice(c * GM_CHUNK, (c + 1) * GM_CHUNK)
            cs = slice(g * gd, (g + 1) * gd)
            mixed = _dot(wg, v[rs, cs]) + bg
            o_ref[rs, cs] = (u[rs, cs] * mixed).astype(o_ref.dtype)


def spatial_gating(proj, v_gain, w_s, b_s, *, bm):
    t = proj.shape[0]
    gd = GM_WIDTH // GM_GROUPS
    b_full = jnp.broadcast_to(b_s[:, :, None], (GM_GROUPS, GM_CHUNK, gd)).astype(F32)
    kernel = functools.partial(_gm_kernel, chunks=bm // GM_CHUNK)
    return pl.pallas_call(
        kernel,
        grid=(t // bm,),
        in_specs=[
            pl.BlockSpec((bm, 2 * GM_WIDTH), lambda i: (i, OFF_GM // (2 * GM_WIDTH))),
            pl.BlockSpec((1, GM_WIDTH), lambda i: (0, 0)),
            pl.BlockSpec((GM_GROUPS, GM_CHUNK, GM_CHUNK), lambda i: (0, 0, 0)),
            pl.BlockSpec((GM_GROUPS, GM_CHUNK, gd), lambda i: (0, 0, 0)),
        ],
        out_specs=pl.BlockSpec((bm, GM_WIDTH), lambda i: (i, 0)),
        out_shape=jax.ShapeDtypeStruct((t, GM_WIDTH), BF16),
        compiler_params=_cparams("parallel"),
        name="spatial_gating",
    )(proj, v_gain.reshape(1, GM_WIDTH), w_s, b_full)


S5_LANE_GROUP = 4
S5_UNROLL = 32


def _s5_lane_chunks():
    n = S5_LANES
    per_half = n // (2 * LANES)
    out = []
    for lc in range(n // LANES):
        hh, k = divmod(lc, per_half)
        out.append((pl.ds(hh * n + k * LANES, LANES), pl.ds(hh * n + n // 2 + k * LANES, LANES)))
    return out


def _cmul_add(pr, pi, sr, si, br, bi):
    return br + (pr * sr - pi * si), bi + (pr * si + pi * sr)


def _s5_kernel(u_ref, perm_ref, permt_ref, wb_ref, pa_ref, pk_ref, wc_ref, d_ref, wg_ref, o_ref,
               x_ref, carry_ref, *, ts):
    n = S5_LANES
    seg = ts // SUBLANES
    chunks = _s5_lane_chunks()

    @pl.when(pl.program_id(1) == 0)
    def _():
        carry_ref[...] = jnp.zeros_like(carry_ref)

    u = _dot(perm_ref[...], u_ref[...]).astype(BF16)
    hw = S5_WIDTH // 2
    for hh in range(2):
        x_ref[:, hh * n:(hh + 1) * n] = _dot(u[:, hh * hw:(hh + 1) * hw], wb_ref[hh])

    row = lax.broadcasted_iota(jnp.int32, (SUBLANES, LANES), 0)
    zero = jnp.zeros((SUBLANES, LANES), F32)
    for g0 in range(0, len(chunks), S5_LANE_GROUP):
        group = chunks[g0:g0 + S5_LANE_GROUP]
        a = [(pa_ref[0, :, re], pa_ref[0, :, im]) for re, im in group]

        def step(k, s, group=group, a=a):
            rows = pl.ds(pl.multiple_of(k * SUBLANES, SUBLANES), SUBLANES)
            out = []
            for (re, im), (ar, ai), (sr, si) in zip(group, a, s):
                nr, ni = _cmul_add(ar, ai, sr, si, x_ref[rows, re], x_ref[rows, im])
                x_ref[rows, re] = nr
                x_ref[rows, im] = ni
                out.append((nr, ni))
            return tuple(out)

        ends = lax.fori_loop(0, seg, step, tuple((zero, zero) for _ in group), unroll=S5_UNROLL)

        enter = []
        for (re, im), (er, ei) in zip(group, ends):
            for tab, shift in ((1, 1), (2, 2), (3, 4)):
                er, ei = _cmul_add(pa_ref[tab, :, re], pa_ref[tab, :, im],
                                   pltpu.roll(er, shift, 0), pltpu.roll(ei, shift, 0), er, ei)
            cr, ci = carry_ref[:, re], carry_ref[:, im]
            er, ei = _cmul_add(pa_ref[4, :, re], pa_ref[4, :, im], cr, ci, er, ei)
            enter.append((jnp.where(row == 0, cr, pltpu.roll(er, 1, 0)),
                          jnp.where(row == 0, ci, pltpu.roll(ei, 1, 0))))
            carry_ref[:, re] = jnp.broadcast_to(er[SUBLANES - 1:, :], (SUBLANES, LANES))
            carry_ref[:, im] = jnp.broadcast_to(ei[SUBLANES - 1:, :], (SUBLANES, LANES))

        def fix(k, c, group=group, enter=enter):
            rows = pl.ds(pl.multiple_of(k * SUBLANES, SUBLANES), SUBLANES)
            for (re, im), (sr, si) in zip(group, enter):
                nr, ni = _cmul_add(pk_ref[k, :, re], pk_ref[k, :, im], sr, si, x_ref[rows, re], x_ref[rows, im])
                x_ref[rows, re] = nr
                x_ref[rows, im] = ni
            return c

        lax.fori_loop(0, seg, fix, 0, unroll=S5_UNROLL)

    y = jnp.concatenate([_dot(x_ref[:, hh * n:(hh + 1) * n].astype(BF16), wc_ref[hh]) for hh in range(2)], axis=-1)
    y = jax.nn.gelu(y + d_ref[...] * u.astype(F32)).astype(BF16)
    ag = _dot(y, wg_ref[...])
    out = (ag[:, :S5_WIDTH] * jax.nn.sigmoid(ag[:, S5_WIDTH:])).astype(BF16)
    o_ref[...] = _dot(permt_ref[...], out).astype(o_ref.dtype)


def _s5_tables(lam_re, lam_im, log_dt, b_re, b_im, c_re, c_im, seg):
    g, n, c = S5_GROUPS, S5_STATE, S5_GROUP_DIM
    lr, li = lam_re.astype(F32), lam_im.astype(F32)
    dt = jnp.exp(log_dt.astype(F32))[:, None]
    mag = jnp.exp(lr * dt)
    a_re, a_im = mag * jnp.cos(li * dt), mag * jnp.sin(li * dt)
    den = lr * lr + li * li
    x_ = a_re - 1.0
    f_re = (x_ * lr + a_im * li) / den
    f_im = (a_im * lr - x_ * li) / den
    br, bi = b_re.astype(F32), b_im.astype(F32)
    bb_re = f_re[..., None] * br - f_im[..., None] * bi
    bb_im = f_re[..., None] * bi + f_im[..., None] * br
    gh = g // 2
    diag = (jnp.arange(gh)[:, None] == jnp.arange(gh)[None, :])[None, :, None, :, None]

    def in_map(bb):
        rows = bb.reshape(2, gh, n, c).transpose(0, 1, 3, 2)[:, :, :, None, :]
        return jnp.where(diag, rows, 0.0).reshape(2, gh * c, gh * n)

    def out_map(cc):
        rows = cc.astype(F32).reshape(2, gh, c, n).transpose(0, 1, 3, 2)[:, :, :, None, :]
        return jnp.where(diag, rows, 0.0).reshape(2, gh * n, gh * c)

    wb = jnp.concatenate([in_map(bb_re), in_map(bb_im)], axis=2).astype(BF16)
    wc = jnp.concatenate([out_map(c_re), -out_map(c_im)], axis=1).astype(BF16)

    def lanes(pr, pi):
        hl = gh * n
        return jnp.concatenate([pr[..., :hl], pi[..., :hl], pr[..., hl:], pi[..., hl:]], axis=-1)

    exps = list(range(1, seg + 1)) + [seg * m for m in (1, 2, 4)] + [seg * (j + 1) for j in range(SUBLANES)]
    sq = [(a_re.reshape(1, -1), a_im.reshape(1, -1))]
    for _ in range(max(exps).bit_length() - 1):
        sr, si = sq[-1]
        sq.append((sr * sr - si * si, 2.0 * sr * si))
    e = jnp.asarray(exps, jnp.int32)[:, None]
    pr, pi = jnp.ones((len(exps), g * n), F32), jnp.zeros((len(exps), g * n), F32)
    for bit, (sr, si) in enumerate(sq):
        on = ((e >> bit) & 1) == 1
        pr, pi = jnp.where(on, pr * sr - pi * si, pr), jnp.where(on, pr * si + pi * sr, pi)
    pw = lanes(pr, pi)
    rows = jnp.arange(SUBLANES)[:, None]
    pa = [jnp.broadcast_to(pw[0][None], (SUBLANES, pw.shape[-1]))]
    for m, shift in enumerate((1, 2, 4)):
        pa.append(jnp.where(rows >= shift, pw[seg + m][None], 0.0))
    pa.append(pw[seg + 3:])
    pk = jnp.broadcast_to(pw[:seg, None, :], (seg, SUBLANES, pw.shape[-1]))
    return wb, wc, jnp.stack(pa).astype(F32), pk.astype(F32)


def s5_mixer(proj, batch, seq, wb, wc, pa, pk, d_skip, w_glu, layer, *, ts):
    t = proj.shape[0]
    nt = seq // ts
    n2 = 2 * S5_LANES
    seg = ts // SUBLANES
    new = jnp.arange(ts)
    perm = (new[:, None] % SUBLANES * seg + new[:, None] // SUBLANES == new[None, :]).astype(BF16)
    kernel = functools.partial(_s5_kernel, ts=ts)
    return pl.pallas_call(
        kernel,
        grid=(batch, nt),
        in_specs=[
            pl.BlockSpec((ts, S5_WIDTH), lambda b, i: (b * nt + i, OFF_S5 // S5_WIDTH)),
            _resident((ts, ts)),
            _resident((ts, ts)),
            _resident((2, S5_WIDTH // 2, S5_LANES), layer),
            _resident((5, SUBLANES, n2), layer),
            _resident((seg, SUBLANES, n2), layer),
            _resident((2, S5_LANES, S5_WIDTH // 2), layer),
            _resident((1, S5_WIDTH), layer),
            _resident((S5_WIDTH, 2 * S5_WIDTH), layer),
        ],
        out_specs=pl.BlockSpec((ts, S5_WIDTH), lambda b, i: (b * nt + i, 0)),
        out_shape=jax.ShapeDtypeStruct((t, S5_WIDTH), BF16),
        scratch_shapes=[pltpu.VMEM((ts, n2), F32), pltpu.VMEM((SUBLANES, n2), F32)],
        compiler_params=_cparams("parallel", "arbitrary"),
        name="s5_mixer",
    )(proj, perm, perm.T, wb, pa, pk, wc, d_skip.reshape(-1, 1, S5_WIDTH).astype(F32), w_glu)


def _merge_kernel(ya_ref, yb_ref, yc_ref, g0_ref, g1_ref, g2_ref, x_ref, wa_ref, wb_ref, wc_ref,
                  wo_ref, gain_ref, o_ref, m_ref, *, cw):
    ya, yb, yc = ya_ref[...], yb_ref[...], yc_ref[...]
    for k in range(m_ref.shape[1] // cw):
        cs = slice(k * cw, (k + 1) * cw)
        m = (jax.nn.sigmoid(g0_ref[:, cs].astype(F32)) * _dot(ya, wa_ref[:, cs])
             + jax.nn.sigmoid(g1_ref[:, cs].astype(F32)) * _dot(yb, wb_ref[:, cs])
             + jax.nn.sigmoid(g2_ref[:, cs].astype(F32)) * _dot(yc, wc_ref[:, cs]))
        m_ref[:, cs] = m.astype(BF16)
    y = _dot(m_ref[...], wo_ref[...])
    o_ref[...] = x_ref[...] + _rms(y, gain_ref[...])


def merge_out(y_a, y_b, y_c, proj, x, w_a, w_b, w_c, w_o, gain, layer, *, bm, cw=256):
    t, d = x.shape
    g0 = OFF_GATE // d
    kernel = functools.partial(_merge_kernel, cw=cw)
    return pl.pallas_call(
        kernel,
        grid=(t // bm,),
        in_specs=[
            pl.BlockSpec((bm, SB_WIDTH), lambda i: (i, 0)),
            pl.BlockSpec((bm, GM_WIDTH), lambda i: (i, 0)),
            pl.BlockSpec((bm, S5_WIDTH), lambda i: (i, 0)),
            pl.BlockSpec((bm, d), lambda i: (i, g0)),
            pl.BlockSpec((bm, d), lambda i: (i, g0 + 1)),
            pl.BlockSpec((bm, d), lambda i: (i, g0 + 2)),
            pl.BlockSpec((bm, d), lambda i: (i, 0)),
            _resident((SB_WIDTH, d), layer),
            _resident((GM_WIDTH, d), layer),
            _resident((S5_WIDTH, d), layer),
            _resident((d, d), layer),
            _resident((1, d)),
        ],
        out_specs=pl.BlockSpec((bm, d), lambda i: (i, 0)),
        out_shape=jax.ShapeDtypeStruct((t, d), F32),
        scratch_shapes=[pltpu.VMEM((bm, d), BF16)],
        compiler_params=_cparams("parallel"),
        name="merge_out",
    )(y_a, y_b, y_c, proj, proj, proj, x, w_a, w_b, w_c, w_o, gain.reshape(1, d))


def _xattn_kernel(x_ref, g_ref, wq_ref, kv_ref, wo_ref, pg_ref, o_ref, *, scale):
    x = x_ref[...]
    h = _rms(x, g_ref[...]).astype(BF16)
    q = (_dot(h, wq_ref[...]) * scale).astype(BF16)
    outs = []
    for hd in range(X_HEADS):
        cs = slice(hd * X_HEAD_DIM, (hd + 1) * X_HEAD_DIM)
        vs = slice(X_WIDTH + hd * X_HEAD_DIM, X_WIDTH + (hd + 1) * X_HEAD_DIM)
        z = _dot_nt(q[:, cs], kv_ref[:, cs])
        e = jnp.exp(z - jnp.max(z, axis=-1, keepdims=True))
        p = e / jnp.sum(e, axis=-1, keepdims=True)
        outs.append(_dot(p.astype(BF16), kv_ref[:, vs]))
    o = jnp.concatenate(outs, axis=-1).astype(BF16)
    o_ref[...] = x + _rms(_dot(o, wo_ref[...]), pg_ref[...])


def cross_attention(x, kv, seq, gain, w_q, w_o, post_gain, layer, *, bm):
    t, d = x.shape
    mem_len = kv.shape[0] // (t // seq)
    per_batch = seq // bm
    kernel = functools.partial(_xattn_kernel, scale=X_HEAD_DIM ** -0.5)
    return pl.pallas_call(
        kernel,
        grid=(t // bm,),
        in_specs=[
            pl.BlockSpec((bm, d), lambda i: (i, 0)),
            pl.BlockSpec((1, d), lambda i: (0, 0)),
            _resident((d, X_WIDTH), layer),
            pl.BlockSpec((mem_len, 2 * X_WIDTH), lambda i: (i // per_batch, 0)),
            _resident((X_WIDTH, d), layer),
            pl.BlockSpec((1, d), lambda i: (0, 0)),
        ],
        out_specs=pl.BlockSpec((bm, d), lambda i: (i, 0)),
        out_shape=jax.ShapeDtypeStruct((t, d), F32),
        compiler_params=_cparams("parallel"),
        name="cross_attention",
    )(x, gain.reshape(1, d), w_q, kv, w_o, post_gain.reshape(1, d))


def _ffn_kernel(x_ref, xn_ref, g_ref, wg_ref, wu_ref, wo_ref, pg_ref, o_ref, h_ref, *, nf, chunk):
    i = pl.program_id(0)
    f = pl.program_id(1)
    slot = lax.rem(i, 2)

    @pl.when(jnp.logical_and(i == 0, f == 0))
    def _():
        h_ref[0] = _rms(x_ref[...], g_ref[...]).astype(BF16)

    @pl.when(f == 0)
    def _():
        o_ref[...] = jnp.zeros_like(o_ref)

    r0 = pl.multiple_of(jnp.minimum(f, x_ref.shape[0] // chunk - 1) * chunk, chunk)
    h_ref[1 - slot, pl.ds(r0, chunk), :] = _rms(xn_ref[pl.ds(r0, chunk), :], g_ref[...]).astype(BF16)

    nb = wg_ref.shape[1] // LANES
    wi = jnp.concatenate(
        [w[:, k * LANES:(k + 1) * LANES] for k in range(nb) for w in (wg_ref, wu_ref)], axis=1)
    gu = _dot(h_ref[slot], wi)
    a = jnp.concatenate(
        [jax.nn.silu(gu[:, 2 * k * LANES:(2 * k + 1) * LANES]) * gu[:, (2 * k + 1) * LANES:(2 * k + 2) * LANES]
         for k in range(nb)], axis=1).astype(BF16)
    o_ref[...] += _dot(a, wo_ref[...])

    @pl.when(f == nf - 1)
    def _():
        o_ref[...] = x_ref[...] + _rms(o_ref[...], pg_ref[...])


def ffn(x, gain, w_in, w_out, post_gain, layer, *, bm, bf):
    t, d = x.shape
    nf = w_out.shape[1] // bf
    n_tiles = t // bm
    chunk = bm // 8 if bm % 128 == 0 and nf >= 8 else bm
    kernel = functools.partial(_ffn_kernel, nf=nf, chunk=chunk)
    return pl.pallas_call(
        kernel,
        grid=(n_tiles, nf),
        in_specs=[
            pl.BlockSpec((bm, d), lambda i, f: (i, 0)),
            pl.BlockSpec((bm, d), lambda i, f: (jnp.minimum(i + 1, n_tiles - 1), 0)),
            pl.BlockSpec((1, d), lambda i, f: (0, 0)),
            pl.BlockSpec((None, d, bf), lambda i, f: (layer, 0, f)),
            pl.BlockSpec((None, d, bf), lambda i, f: (layer, 0, nf + f)),
            pl.BlockSpec((None, bf, d), lambda i, f: (layer, f, 0)),
            pl.BlockSpec((1, d), lambda i, f: (0, 0)),
        ],
        out_specs=pl.BlockSpec((bm, d), lambda i, f: (i, 0)),
        out_shape=jax.ShapeDtypeStruct((t, d), F32),
        scratch_shapes=[pltpu.VMEM((2, bm, d), BF16)],
        compiler_params=_cparams("arbitrary", "arbitrary"),
        name="ffn",
    )(x, x, gain.reshape(1, d), w_in, w_in, w_out, post_gain.reshape(1, d))


def kernel(x, mem, mix_pre_gain, mix_post_gain, w_in, gm_v_gain, gm_w_s, gm_b_s, s5_lam_re, s5_lam_im, s5_log_dt, s5_b_re, s5_b_im, s5_c_re, s5_c_im, s5_d, s5_w_glu, w_br_a, w_br_b, w_br_c, w_out, xattn_pre_gain, xattn_post_gain, mem_gain, w_xq, w_xkv, w_xo, ffn_pre_gain, ffn_post_gain, w_ffn_in, w_ffn_out):
    batch, seq, d = x.shape
    depth = w_in.shape[0]
    t = batch * seq
    bm = math.gcd(seq, 512)
    xs = x.reshape(t, d)
    mem2 = mem.reshape(batch * mem.shape[1], d)
    n_in = w_in.shape[2]
    bn = n_in // 3 if n_in % (3 * LANES) == 0 else n_in
    assert d == D_MODEL and n_in == W_IN_GATE_COL + N_BRANCH * D_MODEL
    w_in16 = jnp.concatenate([w_in[..., W_IN_GATE_COL:], w_in[..., :W_IN_GATE_COL]], axis=-1).astype(BF16)
    w_ffn_in16, w_ffn_out16, w_glu16 = (w.astype(BF16) for w in (w_ffn_in, w_ffn_out, s5_w_glu))
    w_xq16, w_xkv16, w_xo16 = (w.astype(BF16) for w in (w_xq, w_xkv, w_xo))
    w_a16, w_b16, w_c16, w_o16 = (w.astype(BF16) for w in (w_br_a, w_br_b, w_br_c, w_out))
    ts = math.gcd(seq, 256)
    wb, wc, pa, pk = jax.vmap(functools.partial(_s5_tables, seg=ts // SUBLANES))(
        s5_lam_re, s5_lam_im, s5_log_dt, s5_b_re, s5_b_im, s5_c_re, s5_c_im)
    for l in range(depth):
        proj = norm_matmul(xs, mix_pre_gain[l], w_in16, l, bm=bm, bn=bn)
        y_a = sb_attention(proj, batch, seq, bq=math.gcd(seq, 256), bk=256, chains=4)
        y_b = spatial_gating(proj, gm_v_gain[l], gm_w_s[l], gm_b_s[l], bm=bm)
        y_c = s5_mixer(proj, batch, seq, wb, wc, pa, pk, s5_d, w_glu16, l, ts=ts)
        xs = merge_out(y_a, y_b, y_c, proj, xs, w_a16, w_b16, w_c16, w_o16, mix_post_gain[l], l, bm=bm)
        kv = norm_matmul(mem2, mem_gain[l], w_xkv16, l, bm=math.gcd(mem2.shape[0], 256), bn=2 * X_WIDTH)
        xs = cross_attention(xs, kv, seq, xattn_pre_gain[l], w_xq16, w_xo16, xattn_post_gain[l], l, bm=bm)
        xs = ffn(xs, ffn_pre_gain[l], w_ffn_in16, w_ffn_out16, ffn_post_gain[l], l, bm=bm, bf=512)
    return xs.reshape(batch, seq, d)
```

```python
import functools
import math

import jax
import jax.numpy as jnp
from jax import lax
from jax.experimental import pallas as pl
from jax.experimental.pallas import tpu as pltpu

F32 = jnp.float32
BF16 = jnp.bfloat16

RMS_EPS = 1e-6

SB_HEADS = 4
SB_HEAD_DIM = 256
SB_WIDTH = SB_HEADS * SB_HEAD_DIM
GM_GROUPS = 4
GM_CHUNK = 128
GM_WIDTH = 512
S5_WIDTH = 512
S5_GROUP_DIM = 16
S5_GROUPS = S5_WIDTH // S5_GROUP_DIM
S5_STATE = 64
S5_LANES = S5_GROUPS * S5_STATE
X_HEADS = 4
X_HEAD_DIM = 128
X_WIDTH = X_HEADS * X_HEAD_DIM

D_MODEL = 2048
N_BRANCH = 3

W_IN_GATE_COL = 3 * SB_WIDTH + 2 * GM_WIDTH + S5_WIDTH
OFF_GATE = 0
OFF_Q = N_BRANCH * D_MODEL
OFF_K = OFF_Q + SB_WIDTH
OFF_V = OFF_K + SB_WIDTH
OFF_GM = OFF_V + SB_WIDTH
OFF_S5 = OFF_GM + 2 * GM_WIDTH

SUBLANES = 8
LANES = 128

EXP_ZERO_BELOW = -104.0
SB_DEAD_CARRY = -1e30

V7X_VMEM_BYTES = 64 * 1024 * 1024
VMEM_LIMIT = V7X_VMEM_BYTES - 4 * 1024 * 1024


def _cparams(*sem):
    return pltpu.CompilerParams(dimension_semantics=sem, vmem_limit_bytes=VMEM_LIMIT)


def _rms(xf, gain):
    ms = jnp.mean(xf * xf, axis=-1, keepdims=True)
    return xf * lax.rsqrt(ms + RMS_EPS) * gain


def _dot(a, b):
    return jnp.dot(a, b, preferred_element_type=F32)


def _dot_nt(a, b):
    return lax.dot_general(a, b, (((1,), (1,)), ((), ())), preferred_element_type=F32)


def _resident(shape, layer=None):
    if layer is None:
        return pl.BlockSpec(shape, lambda *_: (0,) * len(shape), pipeline_mode=pl.Buffered(1))
    return pl.BlockSpec((None,) + tuple(shape), lambda *_: (layer,) + (0,) * len(shape),
                        pipeline_mode=pl.Buffered(1))


def _norm_matmul_kernel(x_ref, g_ref, w_ref, o_ref, h_ref):
    @pl.when(pl.program_id(1) == 0)
    def _():
        h_ref[...] = _rms(x_ref[...], g_ref[...]).astype(BF16)

    o_ref[...] = _dot(h_ref[...], w_ref[...]).astype(o_ref.dtype)


def norm_matmul(x, gain, w, layer, *, bm, bn):
    t, d = x.shape
    n = w.shape[2]
    return pl.pallas_call(
        _norm_matmul_kernel,
        grid=(t // bm, n // bn),
        in_specs=[
            pl.BlockSpec((bm, d), lambda i, j: (i, 0)),
            pl.BlockSpec((1, d), lambda i, j: (0, 0)),
            pl.BlockSpec((None, d, bn), lambda i, j: (layer, 0, j)),
        ],
        out_specs=pl.BlockSpec((bm, bn), lambda i, j: (i, j)),
        out_shape=jax.ShapeDtypeStruct((t, n), BF16),
        scratch_shapes=[pltpu.VMEM((bm, d), BF16)],
        compiler_params=_cparams("parallel", "arbitrary"),
        name="norm_matmul",
    )(x, gain.reshape(1, d), w)


def _sb_tile(z, mask, carry, v, uo, bk):
    sp = jnp.log(1.0 + jnp.exp(-jnp.abs(z)))
    ls_neg = jnp.minimum(-z, 0.0) - sp
    ls_pos = z + ls_neg
    if mask is not None:
        ls_neg = jnp.where(mask, ls_neg, 0.0)
    hi = ls_neg.astype(BF16)
    lo = (ls_neg - hi.astype(F32)).astype(BF16)
    cs = _dot(hi, uo) + _dot(lo, uo)
    log_w = ls_pos + cs[:, :bk] + jnp.concatenate([carry] * (bk // LANES), axis=1)
    w = jnp.exp(log_w)
    if mask is not None:
        w = jnp.where(mask, w, 0.0)
    return _dot(w.astype(BF16), v), carry + cs[:, bk:]


def _sb_attn_kernel(q_ref, k_ref, v_ref, uo_ref, o_ref, acc_ref, carry_ref, *, bq, bk, chains, scale):
    i = pl.program_id(2)
    uo = uo_ref[...]
    nd = bq // bk
    acc_ref[...] = jnp.zeros_like(acc_ref)
    carry_ref[...] = jnp.zeros_like(carry_ref)

    for c in range(chains):
        for d in reversed(range(nd)):
            rows = slice(c * bq + d * bk, (c + 1) * bq)
            m = bq - d * bk
            off = pl.multiple_of((i * chains + c) * bq + d * bk, bk)
            z = _dot_nt(q_ref[rows, :], k_ref[pl.ds(off, bk), :]) * scale
            row = lax.broadcasted_iota(jnp.int32, (m, bk), 0)
            col = lax.broadcasted_iota(jnp.int32, (m, bk), 1)
            pv, carry = _sb_tile(z, col < row, carry_ref[rows, :], v_ref[pl.ds(off, bk), :], uo, bk)
            acc_ref[rows, :] += pv
            carry_ref[rows, :] = carry

    first = [(i * chains + c) * nd - 1 for c in range(chains)]

    def cond(state):
        t, lives = state
        live = [jnp.logical_and(first[c] - t >= 0, lives[c] > EXP_ZERO_BELOW) for c in range(chains)]
        return functools.reduce(jnp.logical_or, live)

    def body(state):
        t, _ = state
        pvs, lives = [], []
        for c in range(chains):
            rows = slice(c * bq, (c + 1) * bq)
            kb = first[c] - t
            off = pl.multiple_of(jnp.maximum(kb, 0) * bk, bk)
            carry_in = carry_ref[rows, :] + jnp.where(kb < 0, SB_DEAD_CARRY, 0.0)
            z = _dot_nt(q_ref[rows, :], k_ref[pl.ds(off, bk), :]) * scale
            pv, carry = _sb_tile(z, None, carry_in, v_ref[pl.ds(off, bk), :], uo, bk)
            carry_ref[rows, :] = carry
            pvs.append(pv)
            lives.append(jnp.max(carry))
        acc_ref[...] += jnp.concatenate(pvs, axis=0)
        return t + 1, tuple(lives)

    lax.while_loop(cond, body, body((jnp.int32(0), None)))
    o_ref[...] = acc_ref[...].astype(o_ref.dtype)


def sb_attention(proj, batch, seq, *, bq, bk=LANES, chains=1):
    t = proj.shape[0]
    nq = seq // (bq * chains)
    bq_all = bq * chains
    dh = SB_HEAD_DIM
    r = lax.broadcasted_iota(jnp.int32, (bk, bk + LANES), 0)
    c = lax.broadcasted_iota(jnp.int32, (bk, bk + LANES), 1)
    uo = jnp.where(jnp.logical_or(r > c, c >= bk), 1.0, 0.0).astype(BF16)
    kernel = functools.partial(_sb_attn_kernel, bq=bq, bk=bk, chains=chains, scale=dh ** -0.5)
    return pl.pallas_call(
        kernel,
        grid=(batch, SB_HEADS, nq),
        in_specs=[
            pl.BlockSpec((bq_all, dh), lambda b, h, i: (b * nq + i, OFF_Q // dh + h)),
            pl.BlockSpec((seq, dh), lambda b, h, i: (b, OFF_K // dh + h)),
            pl.BlockSpec((seq, dh), lambda b, h, i: (b, OFF_V // dh + h)),
            pl.BlockSpec((bk, bk + LANES), lambda b, h, i: (0, 0)),
        ],
        out_specs=pl.BlockSpec((bq_all, dh), lambda b, h, i: (b * nq + i, h)),
        out_shape=jax.ShapeDtypeStruct((t, SB_WIDTH), BF16),
        scratch_shapes=[pltpu.VMEM((bq_all, dh), F32), pltpu.VMEM((bq_all, LANES), F32)],
        compiler_params=_cparams("parallel", "parallel", "arbitrary"),
        name="sb_attention",
    )(proj, proj, proj, uo)


def _gm_kernel(uv_ref, gain_ref, w_ref, b_ref, o_ref, *, chunks):
    uv = jax.nn.gelu(uv_ref[...].astype(F32))
    u = uv[:, :GM_WIDTH]
    v = _rms(uv[:, GM_WIDTH:], gain_ref[...]).astype(BF16)
    row = lax.broadcasted_iota(jnp.int32, (GM_CHUNK, GM_CHUNK), 0)
    col = lax.broadcasted_iota(jnp.int32, (GM_CHUNK, GM_CHUNK), 1)
    gd = GM_WIDTH // GM_GROUPS
    for g in range(GM_GROUPS):
        wg = jnp.where(col <= row, w_ref[g], 0.0).astype(BF16)
        bg = b_ref[g]
        for c in range(chunks):
            rs = slice(c * GM_CHUNK, (c + 1) * GM_CHUNK)
            cs = slice(g * gd, (g + 1) * gd)
            mixed = _dot(wg, v[rs, cs]) + bg
            o_ref[rs, cs] = (u[rs, cs] * mixed).astype(o_ref.dtype)


def spatial_gating(proj, v_gain, w_s, b_s, *, bm):
    t = proj.shape[0]
    gd = GM_WIDTH // GM_GROUPS
    b_full = jnp.broadcast_to(b_s[:, :, None], (GM_GROUPS, GM_CHUNK, gd)).astype(F32)
    kernel = functools.partial(_gm_kernel, chunks=bm // GM_CHUNK)
    return pl.pallas_call(
        kernel,
        grid=(t // bm,),
        in_specs=[
            pl.BlockSpec((bm, 2 * GM_WIDTH), lambda i: (i, OFF_GM // (2 * GM_WIDTH))),
            pl.BlockSpec((1, GM_WIDTH), lambda i: (0, 0)),
            pl.BlockSpec((GM_GROUPS, GM_CHUNK, GM_CHUNK), lambda i: (0, 0, 0)),
            pl.BlockSpec((GM_GROUPS, GM_CHUNK, gd), lambda i: (0, 0, 0)),
        ],
        out_specs=pl.BlockSpec((bm, GM_WIDTH), lambda i: (i, 0)),
        out_shape=jax.ShapeDtypeStruct((t, GM_WIDTH), BF16),
        compiler_params=_cparams("parallel"),
        name="spatial_gating",
    )(proj, v_gain.reshape(1, GM_WIDTH), w_s, b_full)


S5_LANE_GROUP = 4
S5_UNROLL = 64


def _s5_lane_chunks():
    n = S5_LANES
    per_half = n // (2 * LANES)
    out = []
    for lc in range(n // LANES):
        hh, k = divmod(lc, per_half)
        out.append((pl.ds(hh * n + k * LANES, LANES), pl.ds(hh * n + n // 2 + k * LANES, LANES)))
    return out


def _cmul_add(pr, pi, sr, si, br, bi):
    return br + (pr * sr - pi * si), bi + (pr * si + pi * sr)


def _s5_kernel(u_ref, perm_ref, permt_ref, wb_ref, pa_ref, pk_ref, wc_ref, d_ref, wg_ref, o_ref,
               x_ref, carry_ref, *, ts):
    n = S5_LANES
    seg = ts // SUBLANES
    chunks = _s5_lane_chunks()

    @pl.when(pl.program_id(1) == 0)
    def _():
        carry_ref[...] = jnp.zeros_like(carry_ref)

    u = _dot(perm_ref[...], u_ref[...]).astype(BF16)
    hw = S5_WIDTH // 2
    for hh in range(2):
        x_ref[:, hh * n:(hh + 1) * n] = _dot(u[:, hh * hw:(hh + 1) * hw], wb_ref[hh])

    row = lax.broadcasted_iota(jnp.int32, (SUBLANES, LANES), 0)
    zero = jnp.zeros((SUBLANES, LANES), F32)
    for g0 in range(0, len(chunks), S5_LANE_GROUP):
        group = chunks[g0:g0 + S5_LANE_GROUP]
        a = [(pa_ref[0, :, re], pa_ref[0, :, im]) for re, im in group]

        def step(k, s, group=group, a=a):
            rows = pl.ds(pl.multiple_of(k * SUBLANES, SUBLANES), SUBLANES)
            out = []
            for (re, im), (ar, ai), (sr, si) in zip(group, a, s):
                nr, ni = _cmul_add(ar, ai, sr, si, x_ref[rows, re], x_ref[rows, im])
                x_ref[rows, re] = nr
                x_ref[rows, im] = ni
                out.append((nr, ni))
            return tuple(out)

        ends = lax.fori_loop(0, seg, step, tuple((zero, zero) for _ in group), unroll=S5_UNROLL)

        enter = []
        for (re, im), (er, ei) in zip(group, ends):
            for tab, shift in ((1, 1), (2, 2), (3, 4)):
                er, ei = _cmul_add(pa_ref[tab, :, re], pa_ref[tab, :, im],
                                   pltpu.roll(er, shift, 0), pltpu.roll(ei, shift, 0), er, ei)
            cr, ci = carry_ref[:, re], carry_ref[:, im]
            er, ei = _cmul_add(pa_ref[4, :, re], pa_ref[4, :, im], cr, ci, er, ei)
            enter.append((jnp.where(row == 0, cr, pltpu.roll(er, 1, 0)),
                          jnp.where(row == 0, ci, pltpu.roll(ei, 1, 0))))
            carry_ref[:, re] = jnp.broadcast_to(er[SUBLANES - 1:, :], (SUBLANES, LANES))
            carry_ref[:, im] = jnp.broadcast_to(ei[SUBLANES - 1:, :], (SUBLANES, LANES))

        def fix(k, c, group=group, enter=enter):
            rows = pl.ds(pl.multiple_of(k * SUBLANES, SUBLANES), SUBLANES)
            for (re, im), (sr, si) in zip(group, enter):
                nr, ni = _cmul_add(pk_ref[k, :, re], pk_ref[k, :, im], sr, si, x_ref[rows, re], x_ref[rows, im])
                x_ref[rows, re] = nr
                x_ref[rows, im] = ni
            return c

        lax.fori_loop(0, seg, fix, 0, unroll=S5_UNROLL)

    y = jnp.concatenate([_dot(x_ref[:, hh * n:(hh + 1) * n].astype(BF16), wc_ref[hh]) for hh in range(2)], axis=-1)
    y = jax.nn.gelu(y + d_ref[...] * u.astype(F32)).astype(BF16)
    ag = _dot(y, wg_ref[...])
    out = (ag[:, :S5_WIDTH] * jax.nn.sigmoid(ag[:, S5_WIDTH:])).astype(BF16)
    o_ref[...] = _dot(permt_ref[...], out).astype(o_ref.dtype)


def _s5_tables(lam_re, lam_im, log_dt, b_re, b_im, c_re, c_im, seg):
    g, n, c = S5_GROUPS, S5_STATE, S5_GROUP_DIM
    lr, li = lam_re.astype(F32), lam_im.astype(F32)
    dt = jnp.exp(log_dt.astype(F32))[:, None]
    mag = jnp.exp(lr * dt)
    a_re, a_im = mag * jnp.cos(li * dt), mag * jnp.sin(li * dt)
    den = lr * lr + li * li
    x_ = a_re - 1.0
    f_re = (x_ * lr + a_im * li) / den
    f_im = (a_im * lr - x_ * li) / den
    br, bi = b_re.astype(F32), b_im.astype(F32)
    bb_re = f_re[..., None] * br - f_im[..., None] * bi
    bb_im = f_re[..., None] * bi + f_im[..., None] * br
    gh = g // 2
    eye = jnp.eye(gh, dtype=F32)

    def in_map(bb):
        return jnp.einsum('hgnc,gk->hgckn', bb.reshape(2, gh, n, c), eye).reshape(2, gh * c, gh * n)

    def out_map(cc):
        return jnp.einsum('hgcn,gk->hgnkc', cc.astype(F32).reshape(2, gh, c, n), eye).reshape(2, gh * n, gh * c)

    wb = jnp.concatenate([in_map(bb_re), in_map(bb_im)], axis=2).astype(BF16)
    wc = jnp.concatenate([out_map(c_re), -out_map(c_im)], axis=1).astype(BF16)

    def lanes(pr, pi):
        hl = gh * n
        return jnp.concatenate([pr[..., :hl], pi[..., :hl], pr[..., hl:], pi[..., hl:]], axis=-1)

    exps = list(range(1, seg + 1)) + [seg * m for m in (1, 2, 4)] + [seg * (j + 1) for j in range(SUBLANES)]
    sq = [(a_re.reshape(1, -1), a_im.reshape(1, -1))]
    for _ in range(max(exps).bit_length() - 1):
        sr, si = sq[-1]
        sq.append((sr * sr - si * si, 2.0 * sr * si))
    e = jnp.asarray(exps, jnp.int32)[:, None]
    pr, pi = jnp.ones((len(exps), g * n), F32), jnp.zeros((len(exps), g * n), F32)
    for bit, (sr, si) in enumerate(sq):
        on = ((e >> bit) & 1) == 1
        pr, pi = jnp.where(on, pr * sr - pi * si, pr), jnp.where(on, pr * si + pi * sr, pi)
    pw = lanes(pr, pi)
    rows = jnp.arange(SUBLANES)[:, None]
    pa = [jnp.broadcast_to(pw[0][None], (SUBLANES, pw.shape[-1]))]
    for m, shift in enumerate((1, 2, 4)):
        pa.append(jnp.where(rows >= shift, pw[seg + m][None], 0.0))
    pa.append(pw[seg + 3:])
    pk = jnp.broadcast_to(pw[:seg, None, :], (seg, SUBLANES, pw.shape[-1]))
    return wb, wc, jnp.stack(pa).astype(F32), pk.astype(F32)


def s5_mixer(proj, batch, seq, wb, wc, pa, pk, d_skip, w_glu, layer, *, ts):
    t = proj.shape[0]
    nt = seq // ts
    n2 = 2 * S5_LANES
    seg = ts // SUBLANES
    new = jnp.arange(ts)
    perm = (new[:, None] % SUBLANES * seg + new[:, None] // SUBLANES == new[None, :]).astype(BF16)
    kernel = functools.partial(_s5_kernel, ts=ts)
    return pl.pallas_call(
        kernel,
        grid=(batch, nt),
        in_specs=[
            pl.BlockSpec((ts, S5_WIDTH), lambda b, i: (b * nt + i, OFF_S5 // S5_WIDTH)),
            _resident((ts, ts)),
            _resident((ts, ts)),
            _resident((2, S5_WIDTH // 2, S5_LANES), layer),
            _resident((5, SUBLANES, n2), layer),
            _resident((seg, SUBLANES, n2), layer),
            _resident((2, S5_LANES, S5_WIDTH // 2), layer),
            _resident((1, S5_WIDTH), layer),
            _resident((S5_WIDTH, 2 * S5_WIDTH), layer),
        ],
        out_specs=pl.BlockSpec((ts, S5_WIDTH), lambda b, i: (b * nt + i, 0)),
        out_shape=jax.ShapeDtypeStruct((t, S5_WIDTH), BF16),
        scratch_shapes=[pltpu.VMEM((ts, n2), F32), pltpu.VMEM((SUBLANES, n2), F32)],
        compiler_params=_cparams("parallel", "arbitrary"),
        name="s5_mixer",
    )(proj, perm, perm.T, wb, pa, pk, wc, d_skip.reshape(-1, 1, S5_WIDTH).astype(F32), w_glu)


def _merge_kernel(ya_ref, yb_ref, yc_ref, g0_ref, g1_ref, g2_ref, x_ref, wa_ref, wb_ref, wc_ref,
                  wo_ref, gain_ref, o_ref, m_ref, *, cw):
    ya, yb, yc = ya_ref[...], yb_ref[...], yc_ref[...]
    for k in range(m_ref.shape[1] // cw):
        cs = slice(k * cw, (k + 1) * cw)
        m = (jax.nn.sigmoid(g0_ref[:, cs].astype(F32)) * _dot(ya, wa_ref[:, cs])
             + jax.nn.sigmoid(g1_ref[:, cs].astype(F32)) * _dot(yb, wb_ref[:, cs])
             + jax.nn.sigmoid(g2_ref[:, cs].astype(F32)) * _dot(yc, wc_ref[:, cs]))
        m_ref[:, cs] = m.astype(BF16)
    y = _dot(m_ref[...], wo_ref[...])
    o_ref[...] = x_ref[...] + _rms(y, gain_ref[...])


def merge_out(y_a, y_b, y_c, proj, x, w_a, w_b, w_c, w_o, gain, layer, *, bm, cw=256):
    t, d = x.shape
    g0 = OFF_GATE // d
    kernel = functools.partial(_merge_kernel, cw=cw)
    return pl.pallas_call(
        kernel,
        grid=(t // bm,),
        in_specs=[
            pl.BlockSpec((bm, SB_WIDTH), lambda i: (i, 0)),
            pl.BlockSpec((bm, GM_WIDTH), lambda i: (i, 0)),
            pl.BlockSpec((bm, S5_WIDTH), lambda i: (i, 0)),
            pl.BlockSpec((bm, d), lambda i: (i, g0)),
            pl.BlockSpec((bm, d), lambda i: (i, g0 + 1)),
            pl.BlockSpec((bm, d), lambda i: (i, g0 + 2)),
            pl.BlockSpec((bm, d), lambda i: (i, 0)),
            _resident((SB_WIDTH, d), layer),
            _resident((GM_WIDTH, d), layer),
            _resident((S5_WIDTH, d), layer),
            _resident((d, d), layer),
            _resident((1, d)),
        ],
        out_specs=pl.BlockSpec((bm, d), lambda i: (i, 0)),
        out_shape=jax.ShapeDtypeStruct((t, d), F32),
        scratch_shapes=[pltpu.VMEM((bm, d), BF16)],
        compiler_params=_cparams("parallel"),
        name="merge_out",
    )(y_a, y_b, y_c, proj, proj, proj, x, w_a, w_b, w_c, w_o, gain.reshape(1, d))


def _xattn_kernel(x_ref, g_ref, wq_ref, kv_ref, wo_ref, pg_ref, o_ref, *, scale):
    x = x_ref[...]
    h = _rms(x, g_ref[...]).astype(BF16)
    q = (_dot(h, wq_ref[...]) * scale).astype(BF16)
    outs = []
    for hd in range(X_HEADS):
        cs = slice(hd * X_HEAD_DIM, (hd + 1) * X_HEAD_DIM)
        vs = slice(X_WIDTH + hd * X_HEAD_DIM, X_WIDTH + (hd + 1) * X_HEAD_DIM)
        z = _dot_nt(q[:, cs], kv_ref[:, cs])
        e = jnp.exp(z - jnp.max(z, axis=-1, keepdims=True))
        p = e / jnp.sum(e, axis=-1, keepdims=True)
        outs.append(_dot(p.astype(BF16), kv_ref[:, vs]))
    o = jnp.concatenate(outs, axis=-1).astype(BF16)
    o_ref[...] = x + _rms(_dot(o, wo_ref[...]), pg_ref[...])


def cross_attention(x, kv, seq, gain, w_q, w_o, post_gain, layer, *, bm):
    t, d = x.shape
    mem_len = kv.shape[0] // (t // seq)
    per_batch = seq // bm
    kernel = functools.partial(_xattn_kernel, scale=X_HEAD_DIM ** -0.5)
    return pl.pallas_call(
        kernel,
        grid=(t // bm,),
        in_specs=[
            pl.BlockSpec((bm, d), lambda i: (i, 0)),
            pl.BlockSpec((1, d), lambda i: (0, 0)),
            _resident((d, X_WIDTH), layer),
            pl.BlockSpec((mem_len, 2 * X_WIDTH), lambda i: (i // per_batch, 0)),
            _resident((X_WIDTH, d), layer),
            pl.BlockSpec((1, d), lambda i: (0, 0)),
        ],
        out_specs=pl.BlockSpec((bm, d), lambda i: (i, 0)),
        out_shape=jax.ShapeDtypeStruct((t, d), F32),
        compiler_params=_cparams("parallel"),
        name="cross_attention",
    )(x, gain.reshape(1, d), w_q, kv, w_o, post_gain.reshape(1, d))


def _ffn_kernel(x_ref, g_ref, wg_ref, wu_ref, wo_ref, pg_ref, o_ref, h_ref, *, nf):
    f = pl.program_id(1)

    @pl.when(f == 0)
    def _():
        h_ref[...] = _rms(x_ref[...], g_ref[...]).astype(BF16)
        o_ref[...] = jnp.zeros_like(o_ref)

    nb = wg_ref.shape[1] // LANES
    wi = jnp.concatenate(
        [w[:, k * LANES:(k + 1) * LANES] for k in range(nb) for w in (wg_ref, wu_ref)], axis=1)
    gu = _dot(h_ref[...], wi)
    a = jnp.concatenate(
        [jax.nn.silu(gu[:, 2 * k * LANES:(2 * k + 1) * LANES]) * gu[:, (2 * k + 1) * LANES:(2 * k + 2) * LANES]
         for k in range(nb)], axis=1).astype(BF16)
    o_ref[...] += _dot(a, wo_ref[...])

    @pl.when(f == nf - 1)
    def _():
        o_ref[...] = x_ref[...] + _rms(o_ref[...], pg_ref[...])


def ffn(x, gain, w_in, w_out, post_gain, layer, *, bm, bf):
    t, d = x.shape
    nf = w_out.shape[1] // bf
    kernel = functools.partial(_ffn_kernel, nf=nf)
    return pl.pallas_call(
        kernel,
        grid=(t // bm, nf),
        in_specs=[
            pl.BlockSpec((bm, d), lambda i, f: (i, 0)),
            pl.BlockSpec((1, d), lambda i, f: (0, 0)),
            pl.BlockSpec((None, d, bf), lambda i, f: (layer, 0, f)),
            pl.BlockSpec((None, d, bf), lambda i, f: (layer, 0, nf + f)),
            pl.BlockSpec((None, bf, d), lambda i, f: (layer, f, 0)),
            pl.BlockSpec((1, d), lambda i, f: (0, 0)),
        ],
        out_specs=pl.BlockSpec((bm, d), lambda i, f: (i, 0)),
        out_shape=jax.ShapeDtypeStruct((t, d), F32),
        scratch_shapes=[pltpu.VMEM((bm, d), BF16)],
        compiler_params=_cparams("parallel", "arbitrary"),
        name="ffn",
    )(x, gain.reshape(1, d), w_in, w_in, w_out, post_gain.reshape(1, d))


def kernel(x, mem, mix_pre_gain, mix_post_gain, w_in, gm_v_gain, gm_w_s, gm_b_s, s5_lam_re, s5_lam_im, s5_log_dt, s5_b_re, s5_b_im, s5_c_re, s5_c_im, s5_d, s5_w_glu, w_br_a, w_br_b, w_br_c, w_out, xattn_pre_gain, xattn_post_gain, mem_gain, w_xq, w_xkv, w_xo, ffn_pre_gain, ffn_post_gain, w_ffn_in, w_ffn_out):
    batch, seq, d = x.shape
    depth = w_in.shape[0]
    t = batch * seq
    bm = math.gcd(seq, 512)
    xs = x.reshape(t, d)
    mem2 = mem.reshape(batch * mem.shape[1], d)
    n_in = w_in.shape[2]
    bn = n_in // 3 if n_in % (3 * LANES) == 0 else n_in
    assert d == D_MODEL and n_in == W_IN_GATE_COL + N_BRANCH * D_MODEL
    w_in16 = jnp.concatenate([w_in[..., W_IN_GATE_COL:], w_in[..., :W_IN_GATE_COL]], axis=-1).astype(BF16)
    w_ffn_in16, w_ffn_out16, w_glu16 = (w.astype(BF16) for w in (w_ffn_in, w_ffn_out, s5_w_glu))
    w_xq16, w_xkv16, w_xo16 = (w.astype(BF16) for w in (w_xq, w_xkv, w_xo))
    w_a16, w_b16, w_c16, w_o16 = (w.astype(BF16) for w in (w_br_a, w_br_b, w_br_c, w_out))
    ts = math.gcd(seq, 512)
    wb, wc, pa, pk = jax.vmap(functools.partial(_s5_tables, seg=ts // SUBLANES))(
        s5_lam_re, s5_lam_im, s5_log_dt, s5_b_re, s5_b_im, s5_c_re, s5_c_im)
    for l in range(depth):
        proj = norm_matmul(xs, mix_pre_gain[l], w_in16, l, bm=bm, bn=bn)
        y_a = sb_attention(proj, batch, seq, bq=math.gcd(seq, 256), bk=256, chains=4)
        y_b = spatial_gating(proj, gm_v_gain[l], gm_w_s[l], gm_b_s[l], bm=bm)
        y_c = s5_mixer(proj, batch, seq, wb, wc, pa, pk, s5_d, w_glu16, l, ts=ts)
        xs = merge_out(y_a, y_b, y_c, proj, xs, w_a16, w_b16, w_c16, w_o16, mix_post_gain[l], l, bm=bm)
        kv = norm_matmul(mem2, mem_gain[l], w_xkv16, l, bm=math.gcd(mem2.shape[0], 256), bn=2 * X_WIDTH)
        xs = cross_attention(xs, kv, seq, xattn_pre_gain[l], w_xq16, w_xo16, xattn_post_gain[l], l, bm=bm)
        xs = ffn(xs, ffn_pre_gain[l], w_ffn_in16, w_ffn_out16, ffn_post_gain[l], l, bm=bm, bf=512)
    return xs.reshape(batch, seq, d)
```

```python
import functools
import math

import jax
import jax.numpy as jnp
from jax import lax
from jax.experimental import pallas as pl
from jax.experimental.pallas import tpu as pltpu

F32 = jnp.float32
BF16 = jnp.bfloat16

RMS_EPS = 1e-6

SB_HEADS = 4
SB_HEAD_DIM = 256
SB_WIDTH = SB_HEADS * SB_HEAD_DIM
GM_GROUPS = 4
GM_CHUNK = 128
GM_WIDTH = 512
S5_WIDTH = 512
S5_GROUP_DIM = 16
S5_GROUPS = S5_WIDTH // S5_GROUP_DIM
S5_STATE = 64
S5_LANES = S5_GROUPS * S5_STATE
X_HEADS = 4
X_HEAD_DIM = 128
X_WIDTH = X_HEADS * X_HEAD_DIM

D_MODEL = 2048
N_BRANCH = 3

W_IN_GATE_COL = 3 * SB_WIDTH + 2 * GM_WIDTH + S5_WIDTH
OFF_GATE = 0
OFF_Q = N_BRANCH * D_MODEL
OFF_K = OFF_Q + SB_WIDTH
OFF_V = OFF_K + SB_WIDTH
OFF_GM = OFF_V + SB_WIDTH
OFF_S5 = OFF_GM + 2 * GM_WIDTH

SUBLANES = 8
LANES = 128

EXP_ZERO_BELOW = -104.0
SB_DEAD_CARRY = -1e30

V7X_VMEM_BYTES = 64 * 1024 * 1024
VMEM_LIMIT = V7X_VMEM_BYTES - 4 * 1024 * 1024


def _cparams(*sem):
    return pltpu.CompilerParams(dimension_semantics=sem, vmem_limit_bytes=VMEM_LIMIT)


def _rms(xf, gain):
    ms = jnp.mean(xf * xf, axis=-1, keepdims=True)
    return xf * lax.rsqrt(ms + RMS_EPS) * gain


def _dot(a, b):
    return jnp.dot(a, b, preferred_element_type=F32)


def _dot_nt(a, b):
    return lax.dot_general(a, b, (((1,), (1,)), ((), ())), preferred_element_type=F32)


def _resident(shape, layer=None):
    if layer is None:
        return pl.BlockSpec(shape, lambda *_: (0,) * len(shape), pipeline_mode=pl.Buffered(1))
    return pl.BlockSpec((None,) + tuple(shape), lambda *_: (layer,) + (0,) * len(shape),
                        pipeline_mode=pl.Buffered(1))


def _norm_matmul_kernel(x_ref, g_ref, w_ref, o_ref, h_ref):
    @pl.when(pl.program_id(1) == 0)
    def _():
        h_ref[...] = _rms(x_ref[...], g_ref[...]).astype(BF16)

    o_ref[...] = _dot(h_ref[...], w_ref[...]).astype(o_ref.dtype)


def norm_matmul(x, gain, w, layer, *, bm, bn):
    t, d = x.shape
    n = w.shape[2]
    return pl.pallas_call(
        _norm_matmul_kernel,
        grid=(t // bm, n // bn),
        in_specs=[
            pl.BlockSpec((bm, d), lambda i, j: (i, 0)),
            pl.BlockSpec((1, d), lambda i, j: (0, 0)),
            pl.BlockSpec((None, d, bn), lambda i, j: (layer, 0, j)),
        ],
        out_specs=pl.BlockSpec((bm, bn), lambda i, j: (i, j)),
        out_shape=jax.ShapeDtypeStruct((t, n), BF16),
        scratch_shapes=[pltpu.VMEM((bm, d), BF16)],
        compiler_params=_cparams("parallel", "arbitrary"),
        name="norm_matmul",
    )(x, gain.reshape(1, d), w)


def _sb_tile(z, mask, carry, v, uo, bk):
    sp = jnp.log(1.0 + jnp.exp(-jnp.abs(z)))
    ls_neg = jnp.minimum(-z, 0.0) - sp
    ls_pos = z + ls_neg
    if mask is not None:
        ls_neg = jnp.where(mask, ls_neg, 0.0)
    cs = _dot(ls_neg.astype(BF16), uo)
    log_w = ls_pos + cs[:, :bk] + jnp.concatenate([carry] * (bk // LANES), axis=1)
    w = jnp.exp(log_w)
    if mask is not None:
        w = jnp.where(mask, w, 0.0)
    return _dot(w.astype(BF16), v), carry + cs[:, bk:]


def _sb_attn_kernel(q_ref, k_ref, v_ref, uo_ref, o_ref, acc_ref, carry_ref, *, bq, bk, chains, scale):
    i = pl.program_id(2)
    uo = uo_ref[...]
    nd = bq // bk
    acc_ref[...] = jnp.zeros_like(acc_ref)
    carry_ref[...] = jnp.zeros_like(carry_ref)

    for c in range(chains):
        for d in reversed(range(nd)):
            rows = slice(c * bq + d * bk, (c + 1) * bq)
            m = bq - d * bk
            off = pl.multiple_of((i * chains + c) * bq + d * bk, bk)
            z = _dot_nt(q_ref[rows, :], k_ref[pl.ds(off, bk), :]) * scale
            row = lax.broadcasted_iota(jnp.int32, (m, bk), 0)
            col = lax.broadcasted_iota(jnp.int32, (m, bk), 1)
            pv, carry = _sb_tile(z, col < row, carry_ref[rows, :], v_ref[pl.ds(off, bk), :], uo, bk)
            acc_ref[rows, :] += pv
            carry_ref[rows, :] = carry

    first = [(i * chains + c) * nd - 1 for c in range(chains)]

    def cond(state):
        t, lives = state
        live = [jnp.logical_and(first[c] - t >= 0, lives[c] > EXP_ZERO_BELOW) for c in range(chains)]
        return functools.reduce(jnp.logical_or, live)

    def body(state):
        t, _ = state
        pvs, lives = [], []
        for c in range(chains):
            rows = slice(c * bq, (c + 1) * bq)
            kb = first[c] - t
            off = pl.multiple_of(jnp.maximum(kb, 0) * bk, bk)
            carry_in = carry_ref[rows, :] + jnp.where(kb < 0, SB_DEAD_CARRY, 0.0)
            z = _dot_nt(q_ref[rows, :], k_ref[pl.ds(off, bk), :]) * scale
            pv, carry = _sb_tile(z, None, carry_in, v_ref[pl.ds(off, bk), :], uo, bk)
            carry_ref[rows, :] = carry
            pvs.append(pv)
            lives.append(jnp.max(carry))
        acc_ref[...] += jnp.concatenate(pvs, axis=0)
        return t + 1, tuple(lives)

    lax.while_loop(cond, body, body((jnp.int32(0), None)))
    o_ref[...] = acc_ref[...].astype(o_ref.dtype)


def sb_attention(proj, batch, seq, *, bq, bk=LANES, chains=1):
    t = proj.shape[0]
    nq = seq // (bq * chains)
    bq_all = bq * chains
    dh = SB_HEAD_DIM
    r = lax.broadcasted_iota(jnp.int32, (bk, bk + LANES), 0)
    c = lax.broadcasted_iota(jnp.int32, (bk, bk + LANES), 1)
    uo = jnp.where(jnp.logical_or(r > c, c >= bk), 1.0, 0.0).astype(BF16)
    kernel = functools.partial(_sb_attn_kernel, bq=bq, bk=bk, chains=chains, scale=dh ** -0.5)
    return pl.pallas_call(
        kernel,
        grid=(batch, SB_HEADS, nq),
        in_specs=[
            pl.BlockSpec((bq_all, dh), lambda b, h, i: (b * nq + i, OFF_Q // dh + h)),
            pl.BlockSpec((seq, dh), lambda b, h, i: (b, OFF_K // dh + h)),
            pl.BlockSpec((seq, dh), lambda b, h, i: (b, OFF_V // dh + h)),
            pl.BlockSpec((bk, bk + LANES), lambda b, h, i: (0, 0)),
        ],
        out_specs=pl.BlockSpec((bq_all, dh), lambda b, h, i: (b * nq + i, h)),
        out_shape=jax.ShapeDtypeStruct((t, SB_WIDTH), BF16),
        scratch_shapes=[pltpu.VMEM((bq_all, dh), F32), pltpu.VMEM((bq_all, LANES), F32)],
        compiler_params=_cparams("parallel", "parallel", "arbitrary"),
        name="sb_attention",
    )(proj, proj, proj, uo)


def _gm_kernel(uv_ref, gain_ref, w_ref, b_ref, o_ref, *, chunks):
    uv = jax.nn.gelu(uv_ref[...].astype(F32))
    u = uv[:, :GM_WIDTH]
    v = _rms(uv[:, GM_WIDTH:], gain_ref[...]).astype(BF16)
    row = lax.broadcasted_iota(jnp.int32, (GM_CHUNK, GM_CHUNK), 0)
    col = lax.broadcasted_iota(jnp.int32, (GM_CHUNK, GM_CHUNK), 1)
    gd = GM_WIDTH // GM_GROUPS
    for g in range(GM_GROUPS):
        wg = jnp.where(col <= row, w_ref[g], 0.0).astype(BF16)
        bg = b_ref[g]
        for c in range(chunks):
            rs = slice(c * GM_CHUNK, (c + 1) * GM_CHUNK)
            cs = slice(g * gd, (g + 1) * gd)
            mixed = _dot(wg, v[rs, cs]) + bg
            o_ref[rs, cs] = (u[rs, cs] * mixed).astype(o_ref.dtype)


def spatial_gating(proj, v_gain, w_s, b_s, *, bm):
    t = proj.shape[0]
    gd = GM_WIDTH // GM_GROUPS
    b_full = jnp.broadcast_to(b_s[:, :, None], (GM_GROUPS, GM_CHUNK, gd)).astype(F32)
    kernel = functools.partial(_gm_kernel, chunks=bm // GM_CHUNK)
    return pl.pallas_call(
        kernel,
        grid=(t // bm,),
        in_specs=[
            pl.BlockSpec((bm, 2 * GM_WIDTH), lambda i: (i, OFF_GM // (2 * GM_WIDTH))),
            pl.BlockSpec((1, GM_WIDTH), lambda i: (0, 0)),
            pl.BlockSpec((GM_GROUPS, GM_CHUNK, GM_CHUNK), lambda i: (0, 0, 0)),
            pl.BlockSpec((GM_GROUPS, GM_CHUNK, gd), lambda i: (0, 0, 0)),
        ],
        out_specs=pl.BlockSpec((bm, GM_WIDTH), lambda i: (i, 0)),
        out_shape=jax.ShapeDtypeStruct((t, GM_WIDTH), BF16),
        compiler_params=_cparams("parallel"),
        name="spatial_gating",
    )(proj, v_gain.reshape(1, GM_WIDTH), w_s, b_full)


S5_LANE_GROUP = 4
S5_UNROLL = 64


def _s5_lane_chunks():
    n = S5_LANES
    per_half = n // (2 * LANES)
    out = []
    for lc in range(n // LANES):
        hh, k = divmod(lc, per_half)
        out.append((pl.ds(hh * n + k * LANES, LANES), pl.ds(hh * n + n // 2 + k * LANES, LANES)))
    return out


def _cmul_add(pr, pi, sr, si, br, bi):
    return br + (pr * sr - pi * si), bi + (pr * si + pi * sr)


def _s5_kernel(u_ref, perm_ref, permt_ref, wb_ref, pa_ref, pk_ref, wc_ref, d_ref, wg_ref, o_ref,
               x_ref, carry_ref, *, ts):
    n = S5_LANES
    seg = ts // SUBLANES
    chunks = _s5_lane_chunks()

    @pl.when(pl.program_id(1) == 0)
    def _():
        carry_ref[...] = jnp.zeros_like(carry_ref)

    u = _dot(perm_ref[...], u_ref[...]).astype(BF16)
    hw = S5_WIDTH // 2
    for hh in range(2):
        x_ref[:, hh * n:(hh + 1) * n] = _dot(u[:, hh * hw:(hh + 1) * hw], wb_ref[hh])

    row = lax.broadcasted_iota(jnp.int32, (SUBLANES, LANES), 0)
    zero = jnp.zeros((SUBLANES, LANES), F32)
    for g0 in range(0, len(chunks), S5_LANE_GROUP):
        group = chunks[g0:g0 + S5_LANE_GROUP]
        a = [(pa_ref[0, :, re], pa_ref[0, :, im]) for re, im in group]

        def step(k, s, group=group, a=a):
            rows = pl.ds(pl.multiple_of(k * SUBLANES, SUBLANES), SUBLANES)
            out = []
            for (re, im), (ar, ai), (sr, si) in zip(group, a, s):
                nr, ni = _cmul_add(ar, ai, sr, si, x_ref[rows, re], x_ref[rows, im])
                x_ref[rows, re] = nr
                x_ref[rows, im] = ni
                out.append((nr, ni))
            return tuple(out)

        ends = lax.fori_loop(0, seg, step, tuple((zero, zero) for _ in group), unroll=S5_UNROLL)

        enter = []
        for (re, im), (er, ei) in zip(group, ends):
            for tab, shift in ((1, 1), (2, 2), (3, 4)):
                er, ei = _cmul_add(pa_ref[tab, :, re], pa_ref[tab, :, im],
                                   pltpu.roll(er, shift, 0), pltpu.roll(ei, shift, 0), er, ei)
            cr, ci = carry_ref[:, re], carry_ref[:, im]
            er, ei = _cmul_add(pa_ref[4, :, re], pa_ref[4, :, im], cr, ci, er, ei)
            enter.append((jnp.where(row == 0, cr, pltpu.roll(er, 1, 0)),
                          jnp.where(row == 0, ci, pltpu.roll(ei, 1, 0))))
            carry_ref[:, re] = jnp.broadcast_to(er[SUBLANES - 1:, :], (SUBLANES, LANES))
            carry_ref[:, im] = jnp.broadcast_to(ei[SUBLANES - 1:, :], (SUBLANES, LANES))

        def fix(k, c, group=group, enter=enter):
            rows = pl.ds(pl.multiple_of(k * SUBLANES, SUBLANES), SUBLANES)
            for (re, im), (sr, si) in zip(group, enter):
                nr, ni = _cmul_add(pk_ref[k, :, re], pk_ref[k, :, im], sr, si, x_ref[rows, re], x_ref[rows, im])
                x_ref[rows, re] = nr
                x_ref[rows, im] = ni
            return c

        lax.fori_loop(0, seg, fix, 0, unroll=S5_UNROLL)

    y = jnp.concatenate([_dot(x_ref[:, hh * n:(hh + 1) * n].astype(BF16), wc_ref[hh]) for hh in range(2)], axis=-1)
    y = jax.nn.gelu(y + d_ref[...] * u.astype(F32)).astype(BF16)
    ag = _dot(y, wg_ref[...])
    out = (ag[:, :S5_WIDTH] * jax.nn.sigmoid(ag[:, S5_WIDTH:])).astype(BF16)
    o_ref[...] = _dot(permt_ref[...], out).astype(o_ref.dtype)


def _s5_tables(lam_re, lam_im, log_dt, b_re, b_im, c_re, c_im, seg):
    g, n, c = S5_GROUPS, S5_STATE, S5_GROUP_DIM
    lr, li = lam_re.astype(F32), lam_im.astype(F32)
    dt = jnp.exp(log_dt.astype(F32))[:, None]
    mag = jnp.exp(lr * dt)
    a_re, a_im = mag * jnp.cos(li * dt), mag * jnp.sin(li * dt)
    den = lr * lr + li * li
    x_ = a_re - 1.0
    f_re = (x_ * lr + a_im * li) / den
    f_im = (a_im * lr - x_ * li) / den
    br, bi = b_re.astype(F32), b_im.astype(F32)
    bb_re = f_re[..., None] * br - f_im[..., None] * bi
    bb_im = f_re[..., None] * bi + f_im[..., None] * br
    gh = g // 2
    eye = jnp.eye(gh, dtype=F32)

    def in_map(bb):
        return jnp.einsum('hgnc,gk->hgckn', bb.reshape(2, gh, n, c), eye).reshape(2, gh * c, gh * n)

    def out_map(cc):
        return jnp.einsum('hgcn,gk->hgnkc', cc.astype(F32).reshape(2, gh, c, n), eye).reshape(2, gh * n, gh * c)

    wb = jnp.concatenate([in_map(bb_re), in_map(bb_im)], axis=2).astype(BF16)
    wc = jnp.concatenate([out_map(c_re), -out_map(c_im)], axis=1).astype(BF16)

    def lanes(pr, pi):
        hl = gh * n
        return jnp.concatenate([pr[..., :hl], pi[..., :hl], pr[..., hl:], pi[..., hl:]], axis=-1)

    exps = list(range(1, seg + 1)) + [seg * m for m in (1, 2, 4)] + [seg * (j + 1) for j in range(SUBLANES)]
    sq = [(a_re.reshape(1, -1), a_im.reshape(1, -1))]
    for _ in range(max(exps).bit_length() - 1):
        sr, si = sq[-1]
        sq.append((sr * sr - si * si, 2.0 * sr * si))
    e = jnp.asarray(exps, jnp.int32)[:, None]
    pr, pi = jnp.ones((len(exps), g * n), F32), jnp.zeros((len(exps), g * n), F32)
    for bit, (sr, si) in enumerate(sq):
        on = ((e >> bit) & 1) == 1
        pr, pi = jnp.where(on, pr * sr - pi * si, pr), jnp.where(on, pr * si + pi * sr, pi)
    pw = lanes(pr, pi)
    rows = jnp.arange(SUBLANES)[:, None]
    pa = [jnp.broadcast_to(pw[0][None], (SUBLANES, pw.shape[-1]))]
    for m, shift in enumerate((1, 2, 4)):
        pa.append(jnp.where(rows >= shift, pw[seg + m][None], 0.0))
    pa.append(pw[seg + 3:])
    pk = jnp.broadcast_to(pw[:seg, None, :], (seg, SUBLANES, pw.shape[-1]))
    return wb, wc, jnp.stack(pa).astype(F32), pk.astype(F32)


def s5_mixer(proj, batch, seq, wb, wc, pa, pk, d_skip, w_glu, layer, *, ts):
    t = proj.shape[0]
    nt = seq // ts
    n2 = 2 * S5_LANES
    seg = ts // SUBLANES
    new = jnp.arange(ts)
    perm = (new[:, None] % SUBLANES * seg + new[:, None] // SUBLANES == new[None, :]).astype(BF16)
    kernel = functools.partial(_s5_kernel, ts=ts)
    return pl.pallas_call(
        kernel,
        grid=(batch, nt),
        in_specs=[
            pl.BlockSpec((ts, S5_WIDTH), lambda b, i: (b * nt + i, OFF_S5 // S5_WIDTH)),
            _resident((ts, ts)),
            _resident((ts, ts)),
            _resident((2, S5_WIDTH // 2, S5_LANES), layer),
            _resident((5, SUBLANES, n2), layer),
            _resident((seg, SUBLANES, n2), layer),
            _resident((2, S5_LANES, S5_WIDTH // 2), layer),
            _resident((1, S5_WIDTH), layer),
            _resident((S5_WIDTH, 2 * S5_WIDTH), layer),
        ],
        out_specs=pl.BlockSpec((ts, S5_WIDTH), lambda b, i: (b * nt + i, 0)),
        out_shape=jax.ShapeDtypeStruct((t, S5_WIDTH), BF16),
        scratch_shapes=[pltpu.VMEM((ts, n2), F32), pltpu.VMEM((SUBLANES, n2), F32)],
        compiler_params=_cparams("parallel", "arbitrary"),
        name="s5_mixer",
    )(proj, perm, perm.T, wb, pa, pk, wc, d_skip.reshape(-1, 1, S5_WIDTH).astype(F32), w_glu)


def _merge_kernel(ya_ref, yb_ref, yc_ref, g0_ref, g1_ref, g2_ref, x_ref, wa_ref, wb_ref, wc_ref,
                  wo_ref, gain_ref, o_ref, m_ref, *, cw):
    ya, yb, yc = ya_ref[...], yb_ref[...], yc_ref[...]
    for k in range(m_ref.shape[1] // cw):
        cs = slice(k * cw, (k + 1) * cw)
        m = (jax.nn.sigmoid(g0_ref[:, cs].astype(F32)) * _dot(ya, wa_ref[:, cs])
             + jax.nn.sigmoid(g1_ref[:, cs].astype(F32)) * _dot(yb, wb_ref[:, cs])
             + jax.nn.sigmoid(g2_ref[:, cs].astype(F32)) * _dot(yc, wc_ref[:, cs]))
        m_ref[:, cs] = m.astype(BF16)
    y = _dot(m_ref[...], wo_ref[...])
    o_ref[...] = x_ref[...] + _rms(y, gain_ref[...])


def merge_out(y_a, y_b, y_c, proj, x, w_a, w_b, w_c, w_o, gain, layer, *, bm, cw=256):
    t, d = x.shape
    g0 = OFF_GATE // d
    kernel = functools.partial(_merge_kernel, cw=cw)
    return pl.pallas_call(
        kernel,
        grid=(t // bm,),
        in_specs=[
            pl.BlockSpec((bm, SB_WIDTH), lambda i: (i, 0)),
            pl.BlockSpec((bm, GM_WIDTH), lambda i: (i, 0)),
            pl.BlockSpec((bm, S5_WIDTH), lambda i: (i, 0)),
            pl.BlockSpec((bm, d), lambda i: (i, g0)),
            pl.BlockSpec((bm, d), lambda i: (i, g0 + 1)),
            pl.BlockSpec((bm, d), lambda i: (i, g0 + 2)),
            pl.BlockSpec((bm, d), lambda i: (i, 0)),
            _resident((SB_WIDTH, d), layer),
            _resident((GM_WIDTH, d), layer),
            _resident((S5_WIDTH, d), layer),
            _resident((d, d), layer),
            _resident((1, d)),
        ],
        out_specs=pl.BlockSpec((bm, d), lambda i: (i, 0)),
        out_shape=jax.ShapeDtypeStruct((t, d), F32),
        scratch_shapes=[pltpu.VMEM((bm, d), BF16)],
        compiler_params=_cparams("parallel"),
        name="merge_out",
    )(y_a, y_b, y_c, proj, proj, proj, x, w_a, w_b, w_c, w_o, gain.reshape(1, d))


def _xattn_kernel(x_ref, g_ref, wq_ref, kv_ref, wo_ref, pg_ref, o_ref, *, scale):
    x = x_ref[...]
    h = _rms(x, g_ref[...]).astype(BF16)
    q = (_dot(h, wq_ref[...]) * scale).astype(BF16)
    outs = []
    for hd in range(X_HEADS):
        cs = slice(hd * X_HEAD_DIM, (hd + 1) * X_HEAD_DIM)
        vs = slice(X_WIDTH + hd * X_HEAD_DIM, X_WIDTH + (hd + 1) * X_HEAD_DIM)
        z = _dot_nt(q[:, cs], kv_ref[:, cs])
        e = jnp.exp(z - jnp.max(z, axis=-1, keepdims=True))
        p = e / jnp.sum(e, axis=-1, keepdims=True)
        outs.append(_dot(p.astype(BF16), kv_ref[:, vs]))
    o = jnp.concatenate(outs, axis=-1).astype(BF16)
    o_ref[...] = x + _rms(_dot(o, wo_ref[...]), pg_ref[...])


def cross_attention(x, kv, seq, gain, w_q, w_o, post_gain, layer, *, bm):
    t, d = x.shape
    mem_len = kv.shape[0] // (t // seq)
    per_batch = seq // bm
    kernel = functools.partial(_xattn_kernel, scale=X_HEAD_DIM ** -0.5)
    return pl.pallas_call(
        kernel,
        grid=(t // bm,),
        in_specs=[
            pl.BlockSpec((bm, d), lambda i: (i, 0)),
            pl.BlockSpec((1, d), lambda i: (0, 0)),
            _resident((d, X_WIDTH), layer),
            pl.BlockSpec((mem_len, 2 * X_WIDTH), lambda i: (i // per_batch, 0)),
            _resident((X_WIDTH, d), layer),
            pl.BlockSpec((1, d), lambda i: (0, 0)),
        ],
        out_specs=pl.BlockSpec((bm, d), lambda i: (i, 0)),
        out_shape=jax.ShapeDtypeStruct((t, d), F32),
        compiler_params=_cparams("parallel"),
        name="cross_attention",
    )(x, gain.reshape(1, d), w_q, kv, w_o, post_gain.reshape(1, d))


def _ffn_kernel(x_ref, g_ref, wg_ref, wu_ref, wo_ref, pg_ref, o_ref, h_ref, *, nf):
    f = pl.program_id(1)

    @pl.when(f == 0)
    def _():
        h_ref[...] = _rms(x_ref[...], g_ref[...]).astype(BF16)
        o_ref[...] = jnp.zeros_like(o_ref)

    nb = wg_ref.shape[1] // LANES
    wi = jnp.concatenate(
        [w[:, k * LANES:(k + 1) * LANES] for k in range(nb) for w in (wg_ref, wu_ref)], axis=1)
    gu = _dot(h_ref[...], wi)
    a = jnp.concatenate(
        [jax.nn.silu(gu[:, 2 * k * LANES:(2 * k + 1) * LANES]) * gu[:, (2 * k + 1) * LANES:(2 * k + 2) * LANES]
         for k in range(nb)], axis=1).astype(BF16)
    o_ref[...] += _dot(a, wo_ref[...])

    @pl.when(f == nf - 1)
    def _():
        o_ref[...] = x_ref[...] + _rms(o_ref[...], pg_ref[...])


def ffn(x, gain, w_in, w_out, post_gain, layer, *, bm, bf):
    t, d = x.shape
    nf = w_out.shape[1] // bf
    kernel = functools.partial(_ffn_kernel, nf=nf)
    return pl.pallas_call(
        kernel,
        grid=(t // bm, nf),
        in_specs=[
            pl.BlockSpec((bm, d), lambda i, f: (i, 0)),
            pl.BlockSpec((1, d), lambda i, f: (0, 0)),
            pl.BlockSpec((None, d, bf), lambda i, f: (layer, 0, f)),
            pl.BlockSpec((None, d, bf), lambda i, f: (layer, 0, nf + f)),
            pl.BlockSpec((None, bf, d), lambda i, f: (layer, f, 0)),
            pl.BlockSpec((1, d), lambda i, f: (0, 0)),
        ],
        out_specs=pl.BlockSpec((bm, d), lambda i, f: (i, 0)),
        out_shape=jax.ShapeDtypeStruct((t, d), F32),
        scratch_shapes=[pltpu.VMEM((bm, d), BF16)],
        compiler_params=_cparams("parallel", "arbitrary"),
        name="ffn",
    )(x, gain.reshape(1, d), w_in, w_in, w_out, post_gain.reshape(1, d))


def kernel(x, mem, mix_pre_gain, mix_post_gain, w_in, gm_v_gain, gm_w_s, gm_b_s, s5_lam_re, s5_lam_im, s5_log_dt, s5_b_re, s5_b_im, s5_c_re, s5_c_im, s5_d, s5_w_glu, w_br_a, w_br_b, w_br_c, w_out, xattn_pre_gain, xattn_post_gain, mem_gain, w_xq, w_xkv, w_xo, ffn_pre_gain, ffn_post_gain, w_ffn_in, w_ffn_out):
    batch, seq, d = x.shape
    depth = w_in.shape[0]
    t = batch * seq
    bm = math.gcd(seq, 512)
    xs = x.reshape(t, d)
    mem2 = mem.reshape(batch * mem.shape[1], d)
    n_in = w_in.shape[2]
    bn = n_in // 3 if n_in % (3 * LANES) == 0 else n_in
    assert d == D_MODEL and n_in == W_IN_GATE_COL + N_BRANCH * D_MODEL
    w_in16 = jnp.concatenate([w_in[..., W_IN_GATE_COL:], w_in[..., :W_IN_GATE_COL]], axis=-1).astype(BF16)
    w_ffn_in16, w_ffn_out16, w_glu16 = (w.astype(BF16) for w in (w_ffn_in, w_ffn_out, s5_w_glu))
    w_xq16, w_xkv16, w_xo16 = (w.astype(BF16) for w in (w_xq, w_xkv, w_xo))
    w_a16, w_b16, w_c16, w_o16 = (w.astype(BF16) for w in (w_br_a, w_br_b, w_br_c, w_out))
    ts = math.gcd(seq, 512)
    wb, wc, pa, pk = jax.vmap(functools.partial(_s5_tables, seg=ts // SUBLANES))(
        s5_lam_re, s5_lam_im, s5_log_dt, s5_b_re, s5_b_im, s5_c_re, s5_c_im)
    for l in range(depth):
        proj = norm_matmul(xs, mix_pre_gain[l], w_in16, l, bm=bm, bn=bn)
        y_a = sb_attention(proj, batch, seq, bq=math.gcd(seq, 256), bk=256, chains=4)
        y_b = spatial_gating(proj, gm_v_gain[l], gm_w_s[l], gm_b_s[l], bm=bm)
        y_c = s5_mixer(proj, batch, seq, wb, wc, pa, pk, s5_d, w_glu16, l, ts=ts)
        xs = merge_out(y_a, y_b, y_c, proj, xs, w_a16, w_b16, w_c16, w_o16, mix_post_gain[l], l, bm=bm)
        kv = norm_matmul(mem2, mem_gain[l], w_xkv16, l, bm=math.gcd(mem2.shape[0], 256), bn=2 * X_WIDTH)
        xs = cross_attention(xs, kv, seq, xattn_pre_gain[l], w_xq16, w_xo16, xattn_post_gain[l], l, bm=bm)
        xs = ffn(xs, ffn_pre_gain[l], w_ffn_in16, w_ffn_out16, ffn_post_gain[l], l, bm=bm, bf=512)
    return xs.reshape(batch, seq, d)
```

```python
import functools
import math

import jax
import jax.numpy as jnp
from jax import lax
from jax.experimental import pallas as pl
from jax.experimental.pallas import tpu as pltpu

F32 = jnp.float32
BF16 = jnp.bfloat16

RMS_EPS = 1e-6

SB_HEADS = 4
SB_HEAD_DIM = 256
SB_WIDTH = SB_HEADS * SB_HEAD_DIM
GM_GROUPS = 4
GM_CHUNK = 128
GM_WIDTH = 512
S5_WIDTH = 512
S5_GROUP_DIM = 16
S5_GROUPS = S5_WIDTH // S5_GROUP_DIM
S5_STATE = 64
S5_LANES = S5_GROUPS * S5_STATE
X_HEADS = 4
X_HEAD_DIM = 128
X_WIDTH = X_HEADS * X_HEAD_DIM

D_MODEL = 2048
N_BRANCH = 3

W_IN_GATE_COL = 3 * SB_WIDTH + 2 * GM_WIDTH + S5_WIDTH
OFF_GATE = 0
OFF_Q = N_BRANCH * D_MODEL
OFF_K = OFF_Q + SB_WIDTH
OFF_V = OFF_K + SB_WIDTH
OFF_GM = OFF_V + SB_WIDTH
OFF_S5 = OFF_GM + 2 * GM_WIDTH

SUBLANES = 8
LANES = 128

EXP_ZERO_BELOW = -104.0
SB_DEAD_CARRY = -1e30

V7X_VMEM_BYTES = 64 * 1024 * 1024
VMEM_LIMIT = V7X_VMEM_BYTES - 4 * 1024 * 1024


def _cparams(*sem):
    return pltpu.CompilerParams(dimension_semantics=sem, vmem_limit_bytes=VMEM_LIMIT)


def _rms(xf, gain):
    ms = jnp.mean(xf * xf, axis=-1, keepdims=True)
    return xf * lax.rsqrt(ms + RMS_EPS) * gain


def _dot(a, b):
    return jnp.dot(a, b, preferred_element_type=F32)


def _dot_nt(a, b):
    return lax.dot_general(a, b, (((1,), (1,)), ((), ())), preferred_element_type=F32)


def _resident(shape, layer=None):
    if layer is None:
        return pl.BlockSpec(shape, lambda *_: (0,) * len(shape), pipeline_mode=pl.Buffered(1))
    return pl.BlockSpec((None,) + tuple(shape), lambda *_: (layer,) + (0,) * len(shape),
                        pipeline_mode=pl.Buffered(1))


def _norm_matmul_kernel(x_ref, g_ref, w_ref, o_ref, h_ref):
    @pl.when(pl.program_id(1) == 0)
    def _():
        h_ref[...] = _rms(x_ref[...], g_ref[...]).astype(BF16)

    o_ref[...] = _dot(h_ref[...], w_ref[...]).astype(o_ref.dtype)


def norm_matmul(x, gain, w, layer, *, bm, bn):
    t, d = x.shape
    n = w.shape[2]
    return pl.pallas_call(
        _norm_matmul_kernel,
        grid=(t // bm, n // bn),
        in_specs=[
            pl.BlockSpec((bm, d), lambda i, j: (i, 0)),
            pl.BlockSpec((1, d), lambda i, j: (0, 0)),
            pl.BlockSpec((None, d, bn), lambda i, j: (layer, 0, j)),
        ],
        out_specs=pl.BlockSpec((bm, bn), lambda i, j: (i, j)),
        out_shape=jax.ShapeDtypeStruct((t, n), BF16),
        scratch_shapes=[pltpu.VMEM((bm, d), BF16)],
        compiler_params=_cparams("parallel", "arbitrary"),
        name="norm_matmul",
    )(x, gain.reshape(1, d), w)


def _sb_tile(z, mask, carry, v, uo, bk):
    sp = jnp.log(1.0 + jnp.exp(-jnp.abs(z)))
    ls_neg = jnp.minimum(-z, 0.0) - sp
    ls_pos = z + ls_neg
    if mask is not None:
        ls_neg = jnp.where(mask, ls_neg, 0.0)
    cs = _dot(ls_neg.astype(BF16), uo)
    log_w = ls_pos + cs[:, :bk] + jnp.concatenate([carry] * (bk // LANES), axis=1)
    w = jnp.exp(log_w)
    if mask is not None:
        w = jnp.where(mask, w, 0.0)
    return _dot(w.astype(BF16), v), carry + cs[:, bk:]


def _sb_attn_kernel(q_ref, k_ref, v_ref, uo_ref, o_ref, acc_ref, carry_ref, *, bq, bk, chains, scale):
    i = pl.program_id(2)
    uo = uo_ref[...]
    nd = bq // bk
    acc_ref[...] = jnp.zeros_like(acc_ref)
    carry_ref[...] = jnp.zeros_like(carry_ref)

    for c in range(chains):
        for d in reversed(range(nd)):
            rows = slice(c * bq + d * bk, (c + 1) * bq)
            m = bq - d * bk
            off = pl.multiple_of((i * chains + c) * bq + d * bk, bk)
            z = _dot_nt(q_ref[rows, :], k_ref[pl.ds(off, bk), :]) * scale
            row = lax.broadcasted_iota(jnp.int32, (m, bk), 0)
            col = lax.broadcasted_iota(jnp.int32, (m, bk), 1)
            pv, carry = _sb_tile(z, col < row, carry_ref[rows, :], v_ref[pl.ds(off, bk), :], uo, bk)
            acc_ref[rows, :] += pv
            carry_ref[rows, :] = carry

    first = [(i * chains + c) * nd - 1 for c in range(chains)]

    def cond(state):
        t, lives = state
        live = [jnp.logical_and(first[c] - t >= 0, lives[c] > EXP_ZERO_BELOW) for c in range(chains)]
        return functools.reduce(jnp.logical_or, live)

    def body(state):
        t, _ = state
        pvs, lives = [], []
        for c in range(chains):
            rows = slice(c * bq, (c + 1) * bq)
            kb = first[c] - t
            off = pl.multiple_of(jnp.maximum(kb, 0) * bk, bk)
            carry_in = carry_ref[rows, :] + jnp.where(kb < 0, SB_DEAD_CARRY, 0.0)
            z = _dot_nt(q_ref[rows, :], k_ref[pl.ds(off, bk), :]) * scale
            pv, carry = _sb_tile(z, None, carry_in, v_ref[pl.ds(off, bk), :], uo, bk)
            carry_ref[rows, :] = carry
            pvs.append(pv)
            lives.append(jnp.max(carry))
        acc_ref[...] += jnp.concatenate(pvs, axis=0)
        return t + 1, tuple(lives)

    lax.while_loop(cond, body, body((jnp.int32(0), None)))
    o_ref[...] = acc_ref[...].astype(o_ref.dtype)


def sb_attention(proj, batch, seq, *, bq, bk=LANES, chains=1):
    t = proj.shape[0]
    nq = seq // (bq * chains)
    bq_all = bq * chains
    dh = SB_HEAD_DIM
    r = lax.broadcasted_iota(jnp.int32, (bk, bk + LANES), 0)
    c = lax.broadcasted_iota(jnp.int32, (bk, bk + LANES), 1)
    uo = jnp.where(jnp.logical_or(r > c, c >= bk), 1.0, 0.0).astype(BF16)
    kernel = functools.partial(_sb_attn_kernel, bq=bq, bk=bk, chains=chains, scale=dh ** -0.5)
    return pl.pallas_call(
        kernel,
        grid=(batch, SB_HEADS, nq),
        in_specs=[
            pl.BlockSpec((bq_all, dh), lambda b, h, i: (b * nq + i, OFF_Q // dh + h)),
            pl.BlockSpec((seq, dh), lambda b, h, i: (b, OFF_K // dh + h)),
            pl.BlockSpec((seq, dh), lambda b, h, i: (b, OFF_V // dh + h)),
            pl.BlockSpec((bk, bk + LANES), lambda b, h, i: (0, 0)),
        ],
        out_specs=pl.BlockSpec((bq_all, dh), lambda b, h, i: (b * nq + i, h)),
        out_shape=jax.ShapeDtypeStruct((t, SB_WIDTH), BF16),
        scratch_shapes=[pltpu.VMEM((bq_all, dh), F32), pltpu.VMEM((bq_all, LANES), F32)],
        compiler_params=_cparams("parallel", "parallel", "arbitrary"),
        name="sb_attention",
    )(proj, proj, proj, uo)


def _gm_kernel(uv_ref, gain_ref, w_ref, b_ref, o_ref, *, chunks):
    uv = jax.nn.gelu(uv_ref[...].astype(F32))
    u = uv[:, :GM_WIDTH]
    v = _rms(uv[:, GM_WIDTH:], gain_ref[...]).astype(BF16)
    row = lax.broadcasted_iota(jnp.int32, (GM_CHUNK, GM_CHUNK), 0)
    col = lax.broadcasted_iota(jnp.int32, (GM_CHUNK, GM_CHUNK), 1)
    gd = GM_WIDTH // GM_GROUPS
    for g in range(GM_GROUPS):
        wg = jnp.where(col <= row, w_ref[g], 0.0).astype(BF16)
        bg = b_ref[g]
        for c in range(chunks):
            rs = slice(c * GM_CHUNK, (c + 1) * GM_CHUNK)
            cs = slice(g * gd, (g + 1) * gd)
            mixed = _dot(wg, v[rs, cs]) + bg
            o_ref[rs, cs] = (u[rs, cs] * mixed).astype(o_ref.dtype)


def spatial_gating(proj, v_gain, w_s, b_s, *, bm):
    t = proj.shape[0]
    gd = GM_WIDTH // GM_GROUPS
    b_full = jnp.broadcast_to(b_s[:, :, None], (GM_GROUPS, GM_CHUNK, gd)).astype(F32)
    kernel = functools.partial(_gm_kernel, chunks=bm // GM_CHUNK)
    return pl.pallas_call(
        kernel,
        grid=(t // bm,),
        in_specs=[
            pl.BlockSpec((bm, 2 * GM_WIDTH), lambda i: (i, OFF_GM // (2 * GM_WIDTH))),
            pl.BlockSpec((1, GM_WIDTH), lambda i: (0, 0)),
            pl.BlockSpec((GM_GROUPS, GM_CHUNK, GM_CHUNK), lambda i: (0, 0, 0)),
            pl.BlockSpec((GM_GROUPS, GM_CHUNK, gd), lambda i: (0, 0, 0)),
        ],
        out_specs=pl.BlockSpec((bm, GM_WIDTH), lambda i: (i, 0)),
        out_shape=jax.ShapeDtypeStruct((t, GM_WIDTH), BF16),
        compiler_params=_cparams("parallel"),
        name="spatial_gating",
    )(proj, v_gain.reshape(1, GM_WIDTH), w_s, b_full)


S5_LANE_GROUP = 4
S5_UNROLL = 64


def _s5_lane_chunks():
    n = S5_LANES
    per_half = n // (2 * LANES)
    out = []
    for lc in range(n // LANES):
        hh, k = divmod(lc, per_half)
        out.append((pl.ds(hh * n + k * LANES, LANES), pl.ds(hh * n + n // 2 + k * LANES, LANES)))
    return out


def _cmul_add(pr, pi, sr, si, br, bi):
    return br + (pr * sr - pi * si), bi + (pr * si + pi * sr)


def _s5_kernel(u_ref, perm_ref, permt_ref, wb_ref, pa_ref, pk_ref, wc_ref, d_ref, wg_ref, o_ref,
               x_ref, carry_ref, *, ts):
    n = S5_LANES
    seg = ts // SUBLANES
    chunks = _s5_lane_chunks()

    @pl.when(pl.program_id(1) == 0)
    def _():
        carry_ref[...] = jnp.zeros_like(carry_ref)

    u = _dot(perm_ref[...], u_ref[...]).astype(BF16)
    hw = S5_WIDTH // 2
    for hh in range(2):
        x_ref[:, hh * n:(hh + 1) * n] = _dot(u[:, hh * hw:(hh + 1) * hw], wb_ref[hh])

    row = lax.broadcasted_iota(jnp.int32, (SUBLANES, LANES), 0)
    zero = jnp.zeros((SUBLANES, LANES), F32)
    for g0 in range(0, len(chunks), S5_LANE_GROUP):
        group = chunks[g0:g0 + S5_LANE_GROUP]
        a = [(pa_ref[0, :, re], pa_ref[0, :, im]) for re, im in group]

        def step(k, s, group=group, a=a):
            rows = pl.ds(pl.multiple_of(k * SUBLANES, SUBLANES), SUBLANES)
            out = []
            for (re, im), (ar, ai), (sr, si) in zip(group, a, s):
                nr, ni = _cmul_add(ar, ai, sr, si, x_ref[rows, re], x_ref[rows, im])
                x_ref[rows, re] = nr
                x_ref[rows, im] = ni
                out.append((nr, ni))
            return tuple(out)

        ends = lax.fori_loop(0, seg, step, tuple((zero, zero) for _ in group), unroll=S5_UNROLL)

        enter = []
        for (re, im), (er, ei) in zip(group, ends):
            for tab, shift in ((1, 1), (2, 2), (3, 4)):
                er, ei = _cmul_add(pa_ref[tab, :, re], pa_ref[tab, :, im],
                                   pltpu.roll(er, shift, 0), pltpu.roll(ei, shift, 0), er, ei)
            cr, ci = carry_ref[:, re], carry_ref[:, im]
            er, ei = _cmul_add(pa_ref[4, :, re], pa_ref[4, :, im], cr, ci, er, ei)
            enter.append((jnp.where(row == 0, cr, pltpu.roll(er, 1, 0)),
                          jnp.where(row == 0, ci, pltpu.roll(ei, 1, 0))))
            carry_ref[:, re] = jnp.broadcast_to(er[SUBLANES - 1:, :], (SUBLANES, LANES))
            carry_ref[:, im] = jnp.broadcast_to(ei[SUBLANES - 1:, :], (SUBLANES, LANES))

        def fix(k, c, group=group, enter=enter):
            rows = pl.ds(pl.multiple_of(k * SUBLANES, SUBLANES), SUBLANES)
            for (re, im), (sr, si) in zip(group, enter):
                nr, ni = _cmul_add(pk_ref[k, :, re], pk_ref[k, :, im], sr, si, x_ref[rows, re], x_ref[rows, im])
                x_ref[rows, re] = nr
                x_ref[rows, im] = ni
            return c

        lax.fori_loop(0, seg, fix, 0, unroll=S5_UNROLL)

    y = jnp.concatenate([_dot(x_ref[:, hh * n:(hh + 1) * n].astype(BF16), wc_ref[hh]) for hh in range(2)], axis=-1)
    y = jax.nn.gelu(y + d_ref[...] * u.astype(F32)).astype(BF16)
    ag = _dot(y, wg_ref[...])
    out = (ag[:, :S5_WIDTH] * jax.nn.sigmoid(ag[:, S5_WIDTH:])).astype(BF16)
    o_ref[...] = _dot(permt_ref[...], out).astype(o_ref.dtype)


def _s5_tables(lam_re, lam_im, log_dt, b_re, b_im, c_re, c_im, seg):
    g, n, c = S5_GROUPS, S5_STATE, S5_GROUP_DIM
    lr, li = lam_re.astype(F32), lam_im.astype(F32)
    dt = jnp.exp(log_dt.astype(F32))[:, None]
    mag = jnp.exp(lr * dt)
    a_re, a_im = mag * jnp.cos(li * dt), mag * jnp.sin(li * dt)
    den = lr * lr + li * li
    x_ = a_re - 1.0
    f_re = (x_ * lr + a_im * li) / den
    f_im = (a_im * lr - x_ * li) / den
    br, bi = b_re.astype(F32), b_im.astype(F32)
    bb_re = f_re[..., None] * br - f_im[..., None] * bi
    bb_im = f_re[..., None] * bi + f_im[..., None] * br
    gh = g // 2
    eye = jnp.eye(gh, dtype=F32)

    def in_map(bb):
        return jnp.einsum('hgnc,gk->hgckn', bb.reshape(2, gh, n, c), eye).reshape(2, gh * c, gh * n)

    def out_map(cc):
        return jnp.einsum('hgcn,gk->hgnkc', cc.astype(F32).reshape(2, gh, c, n), eye).reshape(2, gh * n, gh * c)

    wb = jnp.concatenate([in_map(bb_re), in_map(bb_im)], axis=2).astype(BF16)
    wc = jnp.concatenate([out_map(c_re), -out_map(c_im)], axis=1).astype(BF16)

    def lanes(pr, pi):
        hl = gh * n
        return jnp.concatenate([pr[..., :hl], pi[..., :hl], pr[..., hl:], pi[..., hl:]], axis=-1)

    exps = list(range(1, seg + 1)) + [seg * m for m in (1, 2, 4)] + [seg * (j + 1) for j in range(SUBLANES)]
    sq = [(a_re.reshape(1, -1), a_im.reshape(1, -1))]
    for _ in range(max(exps).bit_length() - 1):
        sr, si = sq[-1]
        sq.append((sr * sr - si * si, 2.0 * sr * si))
    e = jnp.asarray(exps, jnp.int32)[:, None]
    pr, pi = jnp.ones((len(exps), g * n), F32), jnp.zeros((len(exps), g * n), F32)
    for bit, (sr, si) in enumerate(sq):
        on = ((e >> bit) & 1) == 1
        pr, pi = jnp.where(on, pr * sr - pi * si, pr), jnp.where(on, pr * si + pi * sr, pi)
    pw = lanes(pr, pi)
    rows = jnp.arange(SUBLANES)[:, None]
    pa = [jnp.broadcast_to(pw[0][None], (SUBLANES, pw.shape[-1]))]
    for m, shift in enumerate((1, 2, 4)):
        pa.append(jnp.where(rows >= shift, pw[seg + m][None], 0.0))
    pa.append(pw[seg + 3:])
    pk = jnp.broadcast_to(pw[:seg, None, :], (seg, SUBLANES, pw.shape[-1]))
    return wb, wc, jnp.stack(pa).astype(F32), pk.astype(F32)


def s5_mixer(proj, batch, seq, wb, wc, pa, pk, d_skip, w_glu, layer, *, ts):
    t = proj.shape[0]
    nt = seq // ts
    n2 = 2 * S5_LANES
    seg = ts // SUBLANES
    new = jnp.arange(ts)
    perm = (new[:, None] % SUBLANES * seg + new[:, None] // SUBLANES == new[None, :]).astype(BF16)
    kernel = functools.partial(_s5_kernel, ts=ts)
    return pl.pallas_call(
        kernel,
        grid=(batch, nt),
        in_specs=[
            pl.BlockSpec((ts, S5_WIDTH), lambda b, i: (b * nt + i, OFF_S5 // S5_WIDTH)),
            _resident((ts, ts)),
            _resident((ts, ts)),
            _resident((2, S5_WIDTH // 2, S5_LANES), layer),
            _resident((5, SUBLANES, n2), layer),
            _resident((seg, SUBLANES, n2), layer),
            _resident((2, S5_LANES, S5_WIDTH // 2), layer),
            _resident((1, S5_WIDTH), layer),
            _resident((S5_WIDTH, 2 * S5_WIDTH), layer),
        ],
        out_specs=pl.BlockSpec((ts, S5_WIDTH), lambda b, i: (b * nt + i, 0)),
        out_shape=jax.ShapeDtypeStruct((t, S5_WIDTH), BF16),
        scratch_shapes=[pltpu.VMEM((ts, n2), F32), pltpu.VMEM((SUBLANES, n2), F32)],
        compiler_params=_cparams("parallel", "arbitrary"),
        name="s5_mixer",
    )(proj, perm, perm.T, wb, pa, pk, wc, d_skip.reshape(-1, 1, S5_WIDTH).astype(F32), w_glu)


def _merge_kernel(ya_ref, yb_ref, yc_ref, g0_ref, g1_ref, g2_ref, x_ref, wa_ref, wb_ref, wc_ref,
                  wo_ref, gain_ref, o_ref, m_ref, *, cw):
    ya, yb, yc = ya_ref[...], yb_ref[...], yc_ref[...]
    for k in range(m_ref.shape[1] // cw):
        cs = slice(k * cw, (k + 1) * cw)
        m = (jax.nn.sigmoid(g0_ref[:, cs].astype(F32)) * _dot(ya, wa_ref[:, cs])
             + jax.nn.sigmoid(g1_ref[:, cs].astype(F32)) * _dot(yb, wb_ref[:, cs])
             + jax.nn.sigmoid(g2_ref[:, cs].astype(F32)) * _dot(yc, wc_ref[:, cs]))
        m_ref[:, cs] = m.astype(BF16)
    y = _dot(m_ref[...], wo_ref[...])
    o_ref[...] = x_ref[...] + _rms(y, gain_ref[...])


def merge_out(y_a, y_b, y_c, proj, x, w_a, w_b, w_c, w_o, gain, layer, *, bm, cw=256):
    t, d = x.shape
    g0 = OFF_GATE // d
    kernel = functools.partial(_merge_kernel, cw=cw)
    return pl.pallas_call(
        kernel,
        grid=(t // bm,),
        in_specs=[
            pl.BlockSpec((bm, SB_WIDTH), lambda i: (i, 0)),
            pl.BlockSpec((bm, GM_WIDTH), lambda i: (i, 0)),
            pl.BlockSpec((bm, S5_WIDTH), lambda i: (i, 0)),
            pl.BlockSpec((bm, d), lambda i: (i, g0)),
            pl.BlockSpec((bm, d), lambda i: (i, g0 + 1)),
            pl.BlockSpec((bm, d), lambda i: (i, g0 + 2)),
            pl.BlockSpec((bm, d), lambda i: (i, 0)),
            _resident((SB_WIDTH, d), layer),
            _resident((GM_WIDTH, d), layer),
            _resident((S5_WIDTH, d), layer),
            _resident((d, d), layer),
            _resident((1, d)),
        ],
        out_specs=pl.BlockSpec((bm, d), lambda i: (i, 0)),
        out_shape=jax.ShapeDtypeStruct((t, d), F32),
        scratch_shapes=[pltpu.VMEM((bm, d), BF16)],
        compiler_params=_cparams("parallel"),
        name="merge_out",
    )(y_a, y_b, y_c, proj, proj, proj, x, w_a, w_b, w_c, w_o, gain.reshape(1, d))


def _xattn_kernel(x_ref, g_ref, wq_ref, kv_ref, wo_ref, pg_ref, o_ref, *, scale):
    x = x_ref[...]
    h = _rms(x, g_ref[...]).astype(BF16)
    q = (_dot(h, wq_ref[...]) * scale).astype(BF16)
    outs = []
    for hd in range(X_HEADS):
        cs = slice(hd * X_HEAD_DIM, (hd + 1) * X_HEAD_DIM)
        vs = slice(X_WIDTH + hd * X_HEAD_DIM, X_WIDTH + (hd + 1) * X_HEAD_DIM)
        z = _dot_nt(q[:, cs], kv_ref[:, cs])
        e = jnp.exp(z - jnp.max(z, axis=-1, keepdims=True))
        p = e / jnp.sum(e, axis=-1, keepdims=True)
        outs.append(_dot(p.astype(BF16), kv_ref[:, vs]))
    o = jnp.concatenate(outs, axis=-1).astype(BF16)
    o_ref[...] = x + _rms(_dot(o, wo_ref[...]), pg_ref[...])


def cross_attention(x, kv, seq, gain, w_q, w_o, post_gain, layer, *, bm):
    t, d = x.shape
    mem_len = kv.shape[0] // (t // seq)
    per_batch = seq // bm
    kernel = functools.partial(_xattn_kernel, scale=X_HEAD_DIM ** -0.5)
    return pl.pallas_call(
        kernel,
        grid=(t // bm,),
        in_specs=[
            pl.BlockSpec((bm, d), lambda i: (i, 0)),
            pl.BlockSpec((1, d), lambda i: (0, 0)),
            _resident((d, X_WIDTH), layer),
            pl.BlockSpec((mem_len, 2 * X_WIDTH), lambda i: (i // per_batch, 0)),
            _resident((X_WIDTH, d), layer),
            pl.BlockSpec((1, d), lambda i: (0, 0)),
        ],
        out_specs=pl.BlockSpec((bm, d), lambda i: (i, 0)),
        out_shape=jax.ShapeDtypeStruct((t, d), F32),
        compiler_params=_cparams("parallel"),
        name="cross_attention",
    )(x, gain.reshape(1, d), w_q, kv, w_o, post_gain.reshape(1, d))


def _ffn_kernel(x_ref, g_ref, wg_ref, wu_ref, wo_ref, pg_ref, o_ref, h_ref, *, nf):
    f = pl.program_id(1)

    @pl.when(f == 0)
    def _():
        h_ref[...] = _rms(x_ref[...], g_ref[...]).astype(BF16)
        o_ref[...] = jnp.zeros_like(o_ref)

    nb = wg_ref.shape[1] // LANES
    wi = jnp.concatenate(
        [w[:, k * LANES:(k + 1) * LANES] for k in range(nb) for w in (wg_ref, wu_ref)], axis=1)
    gu = _dot(h_ref[...], wi)
    a = jnp.concatenate(
        [jax.nn.silu(gu[:, 2 * k * LANES:(2 * k + 1) * LANES]) * gu[:, (2 * k + 1) * LANES:(2 * k + 2) * LANES]
         for k in range(nb)], axis=1).astype(BF16)
    o_ref[...] += _dot(a, wo_ref[...])

    @pl.when(f == nf - 1)
    def _():
        for r in range(0, x_ref.shape[0], SUBLANES):
            rs = slice(r, r + SUBLANES)
            o_ref[rs, :] = x_ref[rs, :] + _rms(o_ref[rs, :], pg_ref[...])


def ffn(x, gain, w_in, w_out, post_gain, layer, *, bm, bf):
    t, d = x.shape
    nf = w_out.shape[1] // bf
    kernel = functools.partial(_ffn_kernel, nf=nf)
    return pl.pallas_call(
        kernel,
        grid=(t // bm, nf),
        in_specs=[
            pl.BlockSpec((bm, d), lambda i, f: (i, 0)),
            pl.BlockSpec((1, d), lambda i, f: (0, 0)),
            pl.BlockSpec((None, d, bf), lambda i, f: (layer, 0, f)),
            pl.BlockSpec((None, d, bf), lambda i, f: (layer, 0, nf + f)),
            pl.BlockSpec((None, bf, d), lambda i, f: (layer, f, 0)),
            pl.BlockSpec((1, d), lambda i, f: (0, 0)),
        ],
        out_specs=pl.BlockSpec((bm, d), lambda i, f: (i, 0)),
        out_shape=jax.ShapeDtypeStruct((t, d), F32),
        scratch_shapes=[pltpu.VMEM((bm, d), BF16)],
        compiler_params=_cparams("parallel", "arbitrary"),
        name="ffn",
    )(x, gain.reshape(1, d), w_in, w_in, w_out, post_gain.reshape(1, d))


def kernel(x, mem, mix_pre_gain, mix_post_gain, w_in, gm_v_gain, gm_w_s, gm_b_s, s5_lam_re, s5_lam_im, s5_log_dt, s5_b_re, s5_b_im, s5_c_re, s5_c_im, s5_d, s5_w_glu, w_br_a, w_br_b, w_br_c, w_out, xattn_pre_gain, xattn_post_gain, mem_gain, w_xq, w_xkv, w_xo, ffn_pre_gain, ffn_post_gain, w_ffn_in, w_ffn_out):
    batch, seq, d = x.shape
    depth = w_in.shape[0]
    t = batch * seq
    bm = math.gcd(seq, 512)
    xs = x.reshape(t, d)
    mem2 = mem.reshape(batch * mem.shape[1], d)
    n_in = w_in.shape[2]
    bn = n_in // 3 if n_in % (3 * LANES) == 0 else n_in
    assert d == D_MODEL and n_in == W_IN_GATE_COL + N_BRANCH * D_MODEL
    w_in16 = jnp.concatenate([w_in[..., W_IN_GATE_COL:], w_in[..., :W_IN_GATE_COL]], axis=-1).astype(BF16)
    w_ffn_in16, w_ffn_out16, w_glu16 = (w.astype(BF16) for w in (w_ffn_in, w_ffn_out, s5_w_glu))
    w_xq16, w_xkv16, w_xo16 = (w.astype(BF16) for w in (w_xq, w_xkv, w_xo))
    w_a16, w_b16, w_c16, w_o16 = (w.astype(BF16) for w in (w_br_a, w_br_b, w_br_c, w_out))
    ts = math.gcd(seq, 512)
    wb, wc, pa, pk = jax.vmap(functools.partial(_s5_tables, seg=ts // SUBLANES))(
        s5_lam_re, s5_lam_im, s5_log_dt, s5_b_re, s5_b_im, s5_c_re, s5_c_im)
    for l in range(depth):
        proj = norm_matmul(xs, mix_pre_gain[l], w_in16, l, bm=bm, bn=bn)
        y_a = sb_attention(proj, batch, seq, bq=math.gcd(seq, 256), bk=256, chains=4)
        y_b = spatial_gating(proj, gm_v_gain[l], gm_w_s[l], gm_b_s[l], bm=bm)
        y_c = s5_mixer(proj, batch, seq, wb, wc, pa, pk, s5_d, w_glu16, l, ts=ts)
        xs = merge_out(y_a, y_b, y_c, proj, xs, w_a16, w_b16, w_c16, w_o16, mix_post_gain[l], l, bm=bm)
        kv = norm_matmul(mem2, mem_gain[l], w_xkv16, l, bm=math.gcd(mem2.shape[0], 256), bn=2 * X_WIDTH)
        xs = cross_attention(xs, kv, seq, xattn_pre_gain[l], w_xq16, w_xo16, xattn_post_gain[l], l, bm=bm)
        xs = ffn(xs, ffn_pre_gain[l], w_ffn_in16, w_ffn_out16, ffn_post_gain[l], l, bm=bm, bf=512)
    return xs.reshape(batch, seq, d)
```
